```python
import jax, jax.numpy as jnp
from jax import lax
import numpy as np

D_MODEL = 4096
BATCH = 4
SEQ = 4096
DEPTH = 4

N_A_LAYERS = DEPTH // 2
N_B_LAYERS = DEPTH - N_A_LAYERS

SSM_GROUP = 16
SSM_N_GROUPS = D_MODEL // SSM_GROUP
SSM_STATE = 64
SSM_CHUNK = 128
DT_MIN = 0.001
DT_MAX = 0.1

MLA_HEADS = 64
Q_RANK = 1024
KV_RANK = 512
QK_NOPE = 128
QK_ROPE = 64
V_DIM = 128
ROPE_THETA = 10000.0
Q_BLOCK = 128

D_FF = 11008
CONV_W = 3
EPS = 1e-6

kernel_name = "yoco_s5_mla_convffn_trunk"


def rms_norm(x, g):
    xf = x.astype(jnp.float32)
    y = xf * lax.rsqrt(jnp.mean(xf * xf, axis=-1, keepdims=True) + EPS)
    return (y * g.astype(jnp.float32)).astype(x.dtype)


def rope(x, positions):
    half = QK_ROPE // 2
    inv_freq = ROPE_THETA ** (-jnp.arange(half, dtype=jnp.float32) / half)
    ang = positions.astype(jnp.float32)[..., None] * inv_freq
    ang = ang.reshape(ang.shape[:2] + (1,) * (x.ndim - 3) + (half,))
    cos, sin = jnp.cos(ang), jnp.sin(ang)
    x1 = x[..., :half].astype(jnp.float32)
    x2 = x[..., half:].astype(jnp.float32)
    out = jnp.concatenate([x1 * cos - x2 * sin, x2 * cos + x1 * sin], axis=-1)
    return out.astype(x.dtype)


def _ssm_combine(e1, e2):
    a1r, a1i, b1r, b1i = e1
    a2r, a2i, b2r, b2i = e2
    return (a2r * a1r - a2i * a1i,
            a2r * a1i + a2i * a1r,
            a2r * b1r - a2i * b1i + b2r,
            a2r * b1i + a2i * b1r + b2i)


def s5_mixer(u, A_re, A_im, log_dt, B_re, B_im, C_re, C_im, D_skip, w_glu, b_glu):
    f32 = jnp.float32
    Bsz, S, _ = u.shape
    G, P, L = SSM_N_GROUPS, SSM_STATE, SSM_CHUNK
    dt = jnp.exp(log_dt.astype(f32))[:, None]
    ar, ai = A_re.astype(f32), A_im.astype(f32)
    mag = jnp.exp(ar * dt)
    lam_re = mag * jnp.cos(ai * dt)
    lam_im = mag * jnp.sin(ai * dt)
    den = ar * ar + ai * ai
    f_re = ((lam_re - 1.0) * ar + lam_im * ai) / den
    f_im = (lam_im * ar - (lam_re - 1.0) * ai) / den
    br, bi = B_re.astype(f32), B_im.astype(f32)
    Bb_re = f_re[..., None] * br - f_im[..., None] * bi
    Bb_im = f_re[..., None] * bi + f_im[..., None] * br
    cr, ci = C_re.astype(f32), C_im.astype(f32)

    n_chunks = S // L
    u_c = u.reshape(Bsz, n_chunks, L, G, SSM_GROUP).transpose(1, 0, 2, 3, 4)
    lam_re_b = jnp.broadcast_to(lam_re, (Bsz, L, G, P))
    lam_im_b = jnp.broadcast_to(lam_im, (Bsz, L, G, P))

    def chunk_step(carry, uc):
        h_re, h_im = carry
        uc = uc.astype(f32)
        bu_re = jnp.einsum('blgc,gpc->blgp', uc, Bb_re)
        bu_im = jnp.einsum('blgc,gpc->blgp', uc, Bb_im)
        a_re, a_im, s_re, s_im = lax.associative_scan(
            _ssm_combine, (lam_re_b, lam_im_b, bu_re, bu_im), axis=1)
        s_re = s_re + a_re * h_re[:, None] - a_im * h_im[:, None]
        s_im = s_im + a_re * h_im[:, None] + a_im * h_re[:, None]
        y = (jnp.einsum('blgp,gcp->blgc', s_re, cr)
             - jnp.einsum('blgp,gcp->blgc', s_im, ci))
        return (s_re[:, -1], s_im[:, -1]), y.reshape(Bsz, L, D_MODEL)

    h0 = (jnp.zeros((Bsz, G, P), f32), jnp.zeros((Bsz, G, P), f32))
    _, ys = lax.scan(chunk_step, h0, u_c)
    y = ys.transpose(1, 0, 2, 3).reshape(Bsz, S, D_MODEL)
    y = y + D_skip.astype(f32) * u.astype(f32)
    y = jax.nn.gelu(y).astype(u.dtype)
    z = y @ w_glu + b_glu
    return z[..., :D_MODEL] * jax.nn.sigmoid(z[..., D_MODEL:])


def conv_ffn(h, w_in, conv_w, conv_b, w_out):
    gu = h @ w_in
    gate, up = gu[..., :D_FF], gu[..., D_FF:]
    gate = lax.conv_general_dilated(
        gate, conv_w[:, None, :], window_strides=(1,), padding=[(CONV_W - 1, 0)],
        dimension_numbers=('NWC', 'WIO', 'NWC'), feature_group_count=D_FF) + conv_b
    return (jax.nn.silu(gate) * up) @ w_out


def mla_shared_kv(x_s, g_in, w_kv_a, g_kv, positions):
    h = rms_norm(x_s, g_in)
    kv = h @ w_kv_a
    c_kv = rms_norm(kv[..., :KV_RANK], g_kv)
    k_rope = rope(kv[..., KV_RANK:], positions)
    return c_kv, k_rope


def mla_layer(h, c_kv, k_rope, w_kv_b, positions, w_q_a, g_q, w_q_b, w_o):
    Bsz, S, _ = h.shape
    c_q = rms_norm(h @ w_q_a, g_q)
    wkv = w_kv_b.reshape(KV_RANK, MLA_HEADS, QK_NOPE + V_DIM)
    w_uk, w_uv = wkv[..., :QK_NOPE], wkv[..., QK_NOPE:]
    n_blk = S // Q_BLOCK
    cq_blk = c_q.reshape(Bsz, n_blk, Q_BLOCK, Q_RANK).transpose(1, 0, 2, 3)
    pos_blk = positions.reshape(Bsz, n_blk, Q_BLOCK).transpose(1, 0, 2)
    key_idx = jnp.arange(S)
    scale = (QK_NOPE + QK_ROPE) ** -0.5

    def attend(args):
        blk, cq, pos = args
        q = (cq @ w_q_b).reshape(Bsz, Q_BLOCK, MLA_HEADS, QK_NOPE + QK_ROPE)
        q_nope = q[..., :QK_NOPE]
        q_pe = rope(q[..., QK_NOPE:], pos)
        q_lat = jnp.einsum('bqhd,chd->bqhc', q_nope, w_uk)
        s = (jnp.einsum('bqhc,bkc->bhqk', q_lat, c_kv, preferred_element_type=jnp.float32)
             + jnp.einsum('bqhr,bkr->bhqk', q_pe, k_rope, preferred_element_type=jnp.float32))
        q_idx = blk * Q_BLOCK + jnp.arange(Q_BLOCK)
        mask = key_idx[None, :] <= q_idx[:, None]
        s = jnp.where(mask, s * scale, -jnp.inf)
        p = jax.nn.softmax(s, axis=-1).astype(c_kv.dtype)
        o_lat = jnp.einsum('bhqk,bkc->bqhc', p, c_kv)
        o = jnp.einsum('bqhc,chd->bqhd', o_lat, w_uv).reshape(Bsz, Q_BLOCK, MLA_HEADS * V_DIM)
        return o @ w_o

    out = lax.map(attend, (jnp.arange(n_blk), cq_blk, pos_blk))
    return out.transpose(1, 0, 2, 3).reshape(Bsz, S, D_MODEL)


def setup_inputs(seed: int = 0) -> dict:
    key = jax.random.key(seed)
    ks = jax.random.split(key, 32)
    f32 = jnp.float32
    nrm = lambda k, shape, s: jax.random.normal(k, shape, f32) * s
    G, P, C = SSM_N_GROUPS, SSM_STATE, SSM_GROUP
    x = jax.random.normal(ks[0], (BATCH, SEQ, D_MODEL), f32)
    offsets = jax.random.randint(ks[1], (BATCH, 1), 0, 1024, dtype=jnp.int32)
    positions = offsets + jnp.arange(SEQ, dtype=jnp.int32)[None, :]
    ssm_A_re = -0.5 + nrm(ks[5], (N_A_LAYERS, G, P), 0.01)
    ssm_A_im = (jnp.pi * jnp.arange(P, dtype=f32))[None, None, :] + nrm(ks[6], (N_A_LAYERS, G, P), 0.01)
    ssm_log_dt = jax.random.uniform(ks[7], (N_A_LAYERS, G), f32, np.log(DT_MIN), np.log(DT_MAX))
    return {
        "x": x,
        "positions": positions,
        "ln_mix": 1.0 + nrm(ks[2], (DEPTH, D_MODEL), 0.02),
        "ln_ffn": 1.0 + nrm(ks[3], (DEPTH, D_MODEL), 0.02),
        "ln_final": 1.0 + nrm(ks[4], (D_MODEL,), 0.02),
        "ssm_A_re": ssm_A_re,
        "ssm_A_im": ssm_A_im,
        "ssm_log_dt": ssm_log_dt,
        "ssm_B_re": nrm(ks[8], (N_A_LAYERS, G, P, C), (2.0 * C) ** -0.5),
        "ssm_B_im": nrm(ks[9], (N_A_LAYERS, G, P, C), (2.0 * C) ** -0.5),
        "ssm_C_re": nrm(ks[10], (N_A_LAYERS, G, C, P), P ** -0.5),
        "ssm_C_im": nrm(ks[11], (N_A_LAYERS, G, C, P), P ** -0.5),
        "ssm_D": nrm(ks[12], (N_A_LAYERS, D_MODEL), 1.0),
        "ssm_w_glu": nrm(ks[13], (N_A_LAYERS, D_MODEL, 2 * D_MODEL), D_MODEL ** -0.5),
        "ssm_b_glu": nrm(ks[14], (N_A_LAYERS, 2 * D_MODEL), 0.02),
        "kv_in_norm": 1.0 + nrm(ks[15], (D_MODEL,), 0.02),
        "w_kv_a": nrm(ks[16], (D_MODEL, KV_RANK + QK_ROPE), D_MODEL ** -0.5),
        "kv_latent_norm": 1.0 + nrm(ks[17], (KV_RANK,), 0.02),
        "w_kv_b": nrm(ks[18], (KV_RANK, MLA_HEADS * (QK_NOPE + V_DIM)), KV_RANK ** -0.5),
        "w_q_a": nrm(ks[19], (N_B_LAYERS, D_MODEL, Q_RANK), D_MODEL ** -0.5),
        "q_latent_norm": 1.0 + nrm(ks[20], (N_B_LAYERS, Q_RANK), 0.02),
        "w_q_b": nrm(ks[21], (N_B_LAYERS, Q_RANK, MLA_HEADS * (QK_NOPE + QK_ROPE)), Q_RANK ** -0.5),
        "w_o": nrm(ks[22], (N_B_LAYERS, MLA_HEADS * V_DIM, D_MODEL), (MLA_HEADS * V_DIM) ** -0.5),
        "ffn_w_in": nrm(ks[23], (DEPTH, D_MODEL, 2 * D_FF), D_MODEL ** -0.5),
        "ffn_conv_w": nrm(ks[24], (DEPTH, CONV_W, D_FF), CONV_W ** -0.5),
        "ffn_conv_b": nrm(ks[25], (DEPTH, D_FF), 0.02),
        "ffn_w_out": nrm(ks[26], (DEPTH, D_FF, D_MODEL), D_FF ** -0.5),
    }


def reference(x, positions, ln_mix, ln_ffn, ln_final,
              ssm_A_re, ssm_A_im, ssm_log_dt, ssm_B_re, ssm_B_im, ssm_C_re, ssm_C_im,
              ssm_D, ssm_w_glu, ssm_b_glu,
              kv_in_norm, w_kv_a, kv_latent_norm, w_kv_b,
              w_q_a, q_latent_norm, w_q_b, w_o,
              ffn_w_in, ffn_conv_w, ffn_conv_b, ffn_w_out):
    c_kv = None
    k_rope = None
    for l in range(DEPTH):
        h = rms_norm(x, ln_mix[l])
        if l < N_A_LAYERS:
            a = l
            x = x + s5_mixer(h, ssm_A_re[a], ssm_A_im[a], ssm_log_dt[a], ssm_B_re[a], ssm_B_im[a],
                             ssm_C_re[a], ssm_C_im[a], ssm_D[a], ssm_w_glu[a], ssm_b_glu[a])
        else:
            b = l - N_A_LAYERS
            x = x + mla_layer(h, c_kv, k_rope, w_kv_b, positions,
                              w_q_a[b], q_latent_norm[b], w_q_b[b], w_o[b])
        x = x + conv_ffn(rms_norm(x, ln_ffn[l]), ffn_w_in[l], ffn_conv_w[l], ffn_conv_b[l], ffn_w_out[l])
        if l == N_A_LAYERS - 1:
            c_kv, k_rope = mla_shared_kv(x, kv_in_norm, w_kv_a, kv_latent_norm, positions)
    return rms_norm(x, ln_final)
```

```python
import functools
import math

import jax
import jax.numpy as jnp
from jax import lax
from jax.experimental import pallas as pl
from jax.experimental.pallas import tpu as pltpu

F32 = jnp.float32
BF16 = jnp.bfloat16

EPS = 1e-6
SSM_GROUP = 16
SSM_STATE = 64
MLA_HEADS = 64
KV_RANK = 512
QK_NOPE = 128
QK_ROPE = 64
V_DIM = 128
ROPE_THETA = 10000.0
CONV_W = 3

VMEM_LIMIT_BYTES = 56 * 1024 * 1024
LANES = 128
SSM_CHUNK = 64
STEPS_PER_LANE_TILE = LANES // SSM_GROUP


def _cparams(*sem):
    return pltpu.CompilerParams(dimension_semantics=sem, vmem_limit_bytes=VMEM_LIMIT_BYTES)


def _tile(dim, pref):
    t = min(dim, pref)
    while dim % t:
        t //= 2
    return t


def _rmsnorm_kernel(x_ref, g_ref, o_ref):
    x = x_ref[...].astype(F32)
    y = x * lax.rsqrt(jnp.mean(x * x, axis=-1, keepdims=True) + EPS)
    o_ref[...] = (y * g_ref[...]).astype(o_ref.dtype)


def rmsnorm(x, g, out_dtype):
    m, d = x.shape
    tm = _tile(m, 512)
    return pl.pallas_call(
        _rmsnorm_kernel,
        grid=(m // tm,),
        in_specs=[pl.BlockSpec((tm, d), lambda i: (i, 0)),
                  pl.BlockSpec((1, d), lambda i: (0, 0))],
        out_specs=pl.BlockSpec((tm, d), lambda i: (i, 0)),
        out_shape=jax.ShapeDtypeStruct((m, d), out_dtype),
        compiler_params=_cparams("parallel"),
        name="rmsnorm",
    )(x, g.reshape(1, d).astype(F32))


def _mm_res_kernel(a_ref, w_ref, r_ref, o_ref, acc_ref):
    k = pl.program_id(2)
    part = jnp.dot(a_ref[...], w_ref[...], preferred_element_type=F32)

    @pl.when(k == 0)
    def _():
        acc_ref[...] = part

    @pl.when(k > 0)
    def _():
        acc_ref[...] += part

    @pl.when(k == pl.num_programs(2) - 1)
    def _():
        o_ref[...] = r_ref[...] + acc_ref[...]


def mm_residual(a, w, res, tm=1024, tn=1024, tk=1024):
    m, kd = a.shape
    n = w.shape[1]
    tm, tn, tk = _tile(m, tm), _tile(n, tn), _tile(kd, tk)
    return pl.pallas_call(
        _mm_res_kernel,
        grid=(m // tm, n // tn, kd // tk),
        in_specs=[pl.BlockSpec((tm, tk), lambda i, j, k: (i, k)),
                  pl.BlockSpec((tk, tn), lambda i, j, k: (k, j)),
                  pl.BlockSpec((tm, tn), lambda i, j, k: (i, j))],
        out_specs=pl.BlockSpec((tm, tn), lambda i, j, k: (i, j)),
        out_shape=jax.ShapeDtypeStruct((m, n), F32),
        scratch_shapes=[pltpu.VMEM((tm, tn), F32)],
        compiler_params=_cparams("parallel", "parallel", "arbitrary"),
        name="mm_residual",
    )(a, w, res)


def _mm_glu_kernel(a_ref, w1_ref, w2_ref, b1_ref, b2_ref, r_ref, o_ref):
    a = a_ref[...]
    z1 = jnp.dot(a, w1_ref[...], preferred_element_type=F32) + b1_ref[...]
    z2 = jnp.dot(a, w2_ref[...], preferred_element_type=F32) + b2_ref[...]
    o_ref[...] = r_ref[...] + z1 * jax.nn.sigmoid(z2)


def mm_glu(a, w, b, res, tm=512, tn=512):
    m, kd = a.shape
    n = w.shape[1] // 2
    tm, tn = _tile(m, tm), _tile(n, tn)
    nj = n // tn
    b2d = b.reshape(1, 2 * n).astype(F32)
    return pl.pallas_call(
        _mm_glu_kernel,
        grid=(m // tm, nj),
        in_specs=[pl.BlockSpec((tm, kd), lambda i, j: (i, 0)),
                  pl.BlockSpec((kd, tn), lambda i, j: (0, j)),
                  pl.BlockSpec((kd, tn), lambda i, j: (0, j + nj)),
                  pl.BlockSpec((1, tn), lambda i, j: (0, j)),
                  pl.BlockSpec((1, tn), lambda i, j: (0, j + nj)),
                  pl.BlockSpec((tm, tn), lambda i, j: (i, j))],
        out_specs=pl.BlockSpec((tm, tn), lambda i, j: (i, j)),
        out_shape=jax.ShapeDtypeStruct((m, n), F32),
        compiler_params=_cparams("parallel", "parallel"),
        name="mm_glu",
    )(a, w, w, b2d, b2d, res)


def _ffn_in_kernel(a_ref, wg_ref, wu_ref, cw_ref, cb_ref, o_ref, carry_ref, *, tiles_per_seq):
    i = pl.program_id(1)
    a = a_ref[...]
    gate = jnp.dot(a, wg_ref[...], preferred_element_type=F32)
    up = jnp.dot(a, wu_ref[...], preferred_element_type=F32)

    @pl.when(i % tiles_per_seq == 0)
    def _():
        carry_ref[...] = jnp.zeros_like(carry_ref)

    prev = carry_ref[...]
    tm = gate.shape[0]
    row = lax.broadcasted_iota(jnp.int32, gate.shape, 0)
    g1 = jnp.where(row == 0, prev[7:8, :], pltpu.roll(gate, 1, 0))
    g2 = jnp.where(row == 0, prev[6:7, :],
                   jnp.where(row == 1, prev[7:8, :], pltpu.roll(gate, 2, 0)))
    carry_ref[...] = gate[tm - 8:, :]
    cw = cw_ref[...]
    conv = cw[0:1, :] * g2 + cw[1:2, :] * g1 + cw[2:3, :] * gate + cb_ref[...]
    o_ref[...] = (jax.nn.silu(conv) * up).astype(o_ref.dtype)


def ffn_in(a, w_in, conv_w, conv_b, seq_len, tm=512, tn=512):
    m, kd = a.shape
    f = w_in.shape[1] // 2
    tm, tn = _tile(min(m, seq_len), tm), _tile(f, tn)
    assert seq_len % tm == 0 and tm >= 8
    nj = f // tn
    kern = functools.partial(_ffn_in_kernel, tiles_per_seq=seq_len // tm)
    return pl.pallas_call(
        kern,
        grid=(nj, m // tm),
        in_specs=[pl.BlockSpec((tm, kd), lambda j, i: (i, 0)),
                  pl.BlockSpec((kd, tn), lambda j, i: (0, j)),
                  pl.BlockSpec((kd, tn), lambda j, i: (0, j + nj)),
                  pl.BlockSpec((CONV_W, tn), lambda j, i: (0, j)),
                  pl.BlockSpec((1, tn), lambda j, i: (0, j))],
        out_specs=pl.BlockSpec((tm, tn), lambda j, i: (i, j)),
        out_shape=jax.ShapeDtypeStruct((m, f), BF16),
        scratch_shapes=[pltpu.VMEM((8, tn), F32)],
        compiler_params=_cparams("parallel", "arbitrary"),
        name="ffn_in",
    )(a, w_in, w_in, conv_w.astype(F32), conv_b.reshape(1, f).astype(F32))


def _cmul(ar, ai, br, bi):
    return ar * br - ai * bi, ar * bi + ai * br


def _s5_tables_kernel(acol_ref, arow_ref, ldt_ref, bt_re_ref, bt_im_ref, btr_re_ref, btr_im_ref,
                      ct_re_ref, ct_im_ref,
                      pt_re_ref, pt_im_ref, q_re_ref, q_im_ref, tblk_ref, laml_ref, *, chunk):
    nblk = chunk // STEPS_PER_LANE_TILE
    dt = jnp.exp(ldt_ref[0])
    ar, ai = acol_ref[0][:, 0:1], acol_ref[0][:, 1:2]
    adr, adi = ar * dt, ai * dt
    p = ar.shape[0]

    def cpow(e, xr, xi):
        mag = jnp.exp(e * xr)
        return mag * jnp.cos(e * xi), mag * jnp.sin(e * xi)

    lane = lax.broadcasted_iota(jnp.int32, (p, LANES), 1)
    kb = (lane // SSM_GROUP).astype(F32)
    eb_re, eb_im = cpow(kb, adr, adi)
    er_re, er_im = cpow((STEPS_PER_LANE_TILE - 1) - kb, adr, adi)
    lk_re, lk_im = cpow(lane.astype(F32), adr, adi)

    def zoh(lr, li, xr, xi):
        den = xr * xr + xi * xi
        return ((lr - 1.0) * xr + li * xi) / den, (li * xr - (lr - 1.0) * xi) / den

    f_re, f_im = zoh(lk_re[:, 1:2], lk_im[:, 1:2], ar, ai)
    bb_re, bb_im = _cmul(f_re, f_im, bt_re_ref[0], bt_im_ref[0])

    arr, air = arow_ref[0][0:1, :], arow_ref[0][1:2, :]
    one = jnp.ones_like(arr)
    lr_row, li_row = cpow(one, arr * dt, air * dt)
    fr_re, fr_im = zoh(lr_row, li_row, arr, air)
    bbt_re, bbt_im = _cmul(fr_re, fr_im, btr_re_ref[0], btr_im_ref[0])
    ll_re, ll_im = cpow(one * float(chunk), arr * dt, air * dt)
    laml_ref[0] = jnp.concatenate([ll_re, ll_im], axis=0)

    ct_re, ct_im = ct_re_ref[0], ct_im_ref[0]
    ktabs = [jnp.zeros((SSM_GROUP, LANES), F32)]
    for j in range(nblk):
        k0 = j * STEPS_PER_LANE_TILE
        kr = chunk - STEPS_PER_LANE_TILE - k0
        e0 = _cmul(eb_re, eb_im, lk_re[:, k0:k0 + 1], lk_im[:, k0:k0 + 1])
        e1 = _cmul(eb_re, eb_im, lk_re[:, k0 + 1:k0 + 2], lk_im[:, k0 + 1:k0 + 2])
        ev = _cmul(er_re, er_im, lk_re[:, kr:kr + 1], lk_im[:, kr:kr + 1])
        sl = slice(j * LANES, (j + 1) * LANES)
        qr, qi = _cmul(e1[0], e1[1], ct_re, ct_im)
        q_re_ref[0, :, sl] = qr.astype(q_re_ref.dtype)
        q_im_ref[0, :, sl] = (-qi).astype(q_im_ref.dtype)
        pr, pi = _cmul(ev[0], ev[1], bb_re, bb_im)
        pt_re_ref[0, :, sl] = pr.astype(pt_re_ref.dtype)
        pt_im_ref[0, :, sl] = pi.astype(pt_im_ref.dtype)
        wr, wi = _cmul(e0[0], e0[1], ct_re, ct_im)
        ktabs.append(jnp.dot(bbt_re, wr, preferred_element_type=F32, precision=lax.Precision.HIGHEST)
                     - jnp.dot(bbt_im, wi, preferred_element_type=F32, precision=lax.Precision.HIGHEST))
    ktab = jnp.concatenate(ktabs, axis=1)
    for s in range(STEPS_PER_LANE_TILE):
        shifted = pltpu.roll(ktab, SSM_GROUP * s, 1) if s else ktab
        for d in range(nblk):
            tblk_ref[0, d, s * SSM_GROUP:(s + 1) * SSM_GROUP, :] = (
                shifted[:, (d + 1) * LANES:(d + 2) * LANES].astype(tblk_ref.dtype))


def s5_tables(a_re, a_im, log_dt, b_re, b_im, c_re, c_im, chunk):
    g, p = a_re.shape
    c = b_re.shape[-1]
    assert c == SSM_GROUP and chunk % STEPS_PER_LANE_TILE == 0 and chunk < LANES
    w = chunk * c
    nblk = chunk // STEPS_PER_LANE_TILE
    rep = LANES // c
    acol = jnp.stack([a_re, a_im], axis=-1).astype(F32)
    arow = jnp.stack([a_re, a_im], axis=1).astype(F32)
    ldt = log_dt.reshape(g, 1, 1).astype(F32)
    bt_re = jnp.tile(b_re.astype(F32), (1, 1, rep))
    bt_im = jnp.tile(b_im.astype(F32), (1, 1, rep))
    btr_re = jnp.swapaxes(b_re, 1, 2).astype(F32)
    btr_im = jnp.swapaxes(b_im, 1, 2).astype(F32)
    ct_re = jnp.tile(jnp.swapaxes(c_re, 1, 2).astype(F32), (1, 1, rep))
    ct_im = jnp.tile(jnp.swapaxes(c_im, 1, 2).astype(F32), (1, 1, rep))
    g3 = lambda s1, s2: pl.BlockSpec((1, s1, s2), lambda i: (i, 0, 0))
    return pl.pallas_call(
        functools.partial(_s5_tables_kernel, chunk=chunk),
        grid=(g,),
        in_specs=[g3(p, 2), g3(2, p), g3(1, 1), g3(p, LANES), g3(p, LANES), g3(c, p), g3(c, p),
                  g3(p, LANES), g3(p, LANES)],
        out_specs=[g3(p, w), g3(p, w), g3(p, w), g3(p, w),
                   pl.BlockSpec((1, nblk, LANES, LANES), lambda i: (i, 0, 0, 0)),
                   g3(2, p)],
        out_shape=[jax.ShapeDtypeStruct((g, p, w), BF16)] * 4
        + [jax.ShapeDtypeStruct((g, nblk, LANES, LANES), BF16),
           jax.ShapeDtypeStruct((g, 2, p), F32)],
        compiler_params=_cparams("parallel"),
        name="s5_tables",
    )(acol, arow, ldt, bt_re, bt_im, btr_re, btr_im, ct_re, ct_im)


def _gelu_tanh(x):
    c = math.sqrt(2.0 / math.pi)
    return x * (0.5 * (1.0 + jnp.tanh(c * (x + 0.044715 * (x * x * x)))))


def _s5_scan_kernel(u_ref, pt_re_ref, pt_im_ref, q_re_ref, q_im_ref, tblk_ref, laml_ref, dsk_ref, o_ref,
                    m_ref, s_re_ref, s_im_ref, h_re_ref, h_im_ref, *, nblk, nchunks, bsz):
    @pl.when(pl.program_id(0) == 0)
    def _():
        m_ref[...] = jnp.zeros_like(m_ref)

    for i in range(nblk):
        for j in range(i, nblk):
            m_ref[i * LANES:(i + 1) * LANES, j * LANES:(j + 1) * LANES] = tblk_ref[0, j - i]

    u = u_ref[0]
    contract_last = (((1,), (1,)), ((), ()))
    s_re_ref[...] = lax.dot_general(u, pt_re_ref[0], contract_last, preferred_element_type=F32)
    s_im_ref[...] = lax.dot_general(u, pt_im_ref[0], contract_last, preferred_element_type=F32)

    lr, li = laml_ref[0][0:1, :], laml_ref[0][1:2, :]
    p = lr.shape[1]
    hr = jnp.zeros((bsz, p), F32)
    hi = jnp.zeros((bsz, p), F32)
    for c in range(nchunks):
        rows = slice(c * bsz, (c + 1) * bsz)
        h_re_ref[rows, :] = hr
        h_im_ref[rows, :] = hi
        hr, hi = (lr * hr - li * hi + s_re_ref[rows, :],
                  lr * hi + li * hr + s_im_ref[rows, :])

    y = jnp.dot(u, m_ref[...], preferred_element_type=F32)
    y += jnp.dot(h_re_ref[...].astype(BF16), q_re_ref[0], preferred_element_type=F32)
    y += jnp.dot(h_im_ref[...].astype(BF16), q_im_ref[0], preferred_element_type=F32)
    y += dsk_ref[0] * u.astype(F32)
    o_ref[0] = _gelu_tanh(y).astype(o_ref.dtype)


def s5_scan(u_g, tables, d_skip_t, bsz):
    pt_re, pt_im, q_re, q_im, tblk, laml = tables
    g, r, w = u_g.shape
    p = pt_re.shape[1]
    nblk = tblk.shape[1]
    g3 = lambda s1, s2: pl.BlockSpec((1, s1, s2), lambda i: (i, 0, 0))
    kern = functools.partial(_s5_scan_kernel, nblk=nblk, nchunks=r // bsz, bsz=bsz)
    return pl.pallas_call(
        kern,
        grid=(g,),
        in_specs=[g3(r, w), g3(p, w), g3(p, w), g3(p, w), g3(p, w),
                  pl.BlockSpec((1, nblk, LANES, LANES), lambda i: (i, 0, 0, 0)),
                  g3(2, p), g3(1, w)],
        out_specs=g3(r, w),
        out_shape=jax.ShapeDtypeStruct((g, r, w), BF16),
        scratch_shapes=[pltpu.VMEM((w, w), BF16)] + [pltpu.VMEM((r, p), F32)] * 4,
        compiler_params=_cparams("arbitrary"),
        name="s5_scan",
    )(u_g, pt_re, pt_im, q_re, q_im, tblk, laml, d_skip_t)


def s5_mixer_pre_glu(h, bsz, seq_len, a_re, a_im, log_dt, b_re, b_im, c_re, c_im, d_skip):
    t, d = h.shape
    g = d // SSM_GROUP
    chunk = min(SSM_CHUNK, seq_len)
    nc = seq_len // chunk
    tables = s5_tables(a_re, a_im, log_dt, b_re, b_im, c_re, c_im, chunk)
    u_g = (h.reshape(bsz, nc, chunk, g, SSM_GROUP).transpose(3, 1, 0, 2, 4)
           .reshape(g, nc * bsz, chunk * SSM_GROUP))
    d_t = jnp.tile(d_skip.reshape(g, 1, SSM_GROUP).astype(F32), (1, 1, chunk))
    y_g = s5_scan(u_g, tables, d_t, bsz)
    return (y_g.reshape(g, nc, bsz, chunk, SSM_GROUP).transpose(2, 1, 3, 0, 4).reshape(t, d))


def _rope_tables_kernel(pos_ref, freq_ref, sign_ref, cos_ref, sin_ref):
    ang = pos_ref[...] * freq_ref[...]
    cos_ref[...] = jnp.cos(ang)
    sin_ref[...] = jnp.sin(ang) * sign_ref[...]


def rope_tables(positions):
    t = positions.size
    half = QK_ROPE // 2
    inv_freq = ROPE_THETA ** (-jnp.arange(half, dtype=F32) / half)
    freq = jnp.tile(inv_freq, LANES // half).reshape(1, LANES)
    sign = jnp.tile(jnp.concatenate([-jnp.ones(half, F32), jnp.ones(half, F32)]),
                    LANES // QK_ROPE).reshape(1, LANES)
    tm = _tile(t, 2048)
    row = pl.BlockSpec((1, LANES), lambda i: (0, 0))
    return pl.pallas_call(
        _rope_tables_kernel,
        grid=(t // tm,),
        in_specs=[pl.BlockSpec((tm, 1), lambda i: (i, 0)), row, row],
        out_specs=[pl.BlockSpec((tm, LANES), lambda i: (i, 0))] * 2,
        out_shape=[jax.ShapeDtypeStruct((t, LANES), F32)] * 2,
        compiler_params=_cparams("parallel"),
        name="rope_tables",
    )(positions.reshape(t, 1).astype(F32), freq, sign)


def _rotate_half_pairs(x):
    n = x.shape[-1]
    lane = lax.broadcasted_iota(jnp.int32, x.shape, x.ndim - 1)
    half = QK_ROPE // 2
    return jnp.where(lane % QK_ROPE < half, pltpu.roll(x, n - half, x.ndim - 1),
                     pltpu.roll(x, half, x.ndim - 1))


def _kv_a_kernel(a_ref, w_ref, g_ref, cos_ref, sin_ref, ckv_ref, krope_ref):
    acc = jnp.dot(a_ref[...], w_ref[...], preferred_element_type=F32)
    c = acc[:, :KV_RANK]
    c = c * lax.rsqrt(jnp.mean(c * c, axis=-1, keepdims=True) + EPS)
    ckv_ref[...] = (c * g_ref[...]).astype(ckv_ref.dtype)
    r = acc[:, KV_RANK:]
    r = r * cos_ref[...] + _rotate_half_pairs(r) * sin_ref[...]
    krope_ref[...] = r[:, :QK_ROPE].astype(krope_ref.dtype)


def kv_a(a, w_kv_a, g_kv, cos_t, sin_t, tm=512):
    m, kd = a.shape
    n = KV_RANK + LANES
    w = jnp.pad(w_kv_a, ((0, 0), (0, n - w_kv_a.shape[1]))).astype(BF16)
    tm = _tile(m, tm)
    return pl.pallas_call(
        _kv_a_kernel,
        grid=(m // tm,),
        in_specs=[pl.BlockSpec((tm, kd), lambda i: (i, 0)),
                  pl.BlockSpec((kd, n), lambda i: (0, 0)),
                  pl.BlockSpec((1, KV_RANK), lambda i: (0, 0)),
                  pl.BlockSpec((tm, LANES), lambda i: (i, 0)),
                  pl.BlockSpec((tm, LANES), lambda i: (i, 0))],
        out_specs=[pl.BlockSpec((tm, KV_RANK), lambda i: (i, 0)),
                   pl.BlockSpec((tm, QK_ROPE), lambda i: (i, 0))],
        out_shape=[jax.ShapeDtypeStruct((m, KV_RANK), BF16),
                   jax.ShapeDtypeStruct((m, QK_ROPE), BF16)],
        compiler_params=_cparams("parallel"),
        name="kv_a",
    )(a, w, g_kv.reshape(1, KV_RANK).astype(F32), cos_t, sin_t)


def _kv_b_kernel(c_ref, kr_ref, w_ref, k_ref, v_ref, *, hb):
    acc = jnp.dot(c_ref[...], w_ref[...], preferred_element_type=F32)
    kr = kr_ref[...]
    hw = QK_NOPE + V_DIM
    for h in range(hb):
        k_ref[h, :, :QK_NOPE] = acc[:, h * hw:h * hw + QK_NOPE].astype(k_ref.dtype)
        k_ref[h, :, QK_NOPE:] = kr
        v_ref[h] = acc[:, h * hw + QK_NOPE:(h + 1) * hw].astype(v_ref.dtype)


def kv_b(ckv, krope, w_kv_b, hb=8, tm=1024):
    m = ckv.shape[0]
    hw = QK_NOPE + V_DIM
    heads = w_kv_b.shape[1] // hw
    hb = min(hb, heads)
    tm = _tile(m, tm)
    return pl.pallas_call(
        functools.partial(_kv_b_kernel, hb=hb),
        grid=(m // tm, heads // hb),
        in_specs=[pl.BlockSpec((tm, KV_RANK), lambda i, j: (i, 0)),
                  pl.BlockSpec((tm, QK_ROPE), lambda i, j: (i, 0)),
                  pl.BlockSpec((KV_RANK, hb * hw), lambda i, j: (0, j))],
        out_specs=[pl.BlockSpec((hb, tm, QK_NOPE + QK_ROPE), lambda i, j: (j, i, 0)),
                   pl.BlockSpec((hb, tm, V_DIM), lambda i, j: (j, i, 0))],
        out_shape=[jax.ShapeDtypeStruct((heads, m, QK_NOPE + QK_ROPE), BF16),
                   jax.ShapeDtypeStruct((heads, m, V_DIM), BF16)],
        compiler_params=_cparams("parallel", "parallel"),
        name="kv_b",
    )(ckv, krope, w_kv_b)


def _q_a_kernel(a_ref, w_ref, g_ref, o_ref):
    c = jnp.dot(a_ref[...], w_ref[...], preferred_element_type=F32)
    c = c * lax.rsqrt(jnp.mean(c * c, axis=-1, keepdims=True) + EPS)
    o_ref[...] = (c * g_ref[...]).astype(o_ref.dtype)


def q_a(a, w, g, tm=512):
    m, kd = a.shape
    n = w.shape[1]
    tm = _tile(m, tm)
    return pl.pallas_call(
        _q_a_kernel,
        grid=(m // tm,),
        in_specs=[pl.BlockSpec((tm, kd), lambda i: (i, 0)),
                  pl.BlockSpec((kd, n), lambda i: (0, 0)),
                  pl.BlockSpec((1, n), lambda i: (0, 0))],
        out_specs=pl.BlockSpec((tm, n), lambda i: (i, 0)),
        out_shape=jax.ShapeDtypeStruct((m, n), BF16),
        compiler_params=_cparams("parallel"),
        name="q_a",
    )(a, w, g.reshape(1, n).astype(F32))


def _q_b_kernel(c_ref, w_ref, cos_ref, sin_ref, o_ref, *, hb, scale):
    acc = jnp.dot(c_ref[...], w_ref[...], preferred_element_type=F32)
    nope = acc[:, :hb * QK_NOPE] * scale
    pe = acc[:, hb * QK_NOPE:]
    reps = hb * QK_ROPE // LANES
    cos = jnp.concatenate([cos_ref[...]] * reps, axis=1)
    sin = jnp.concatenate([sin_ref[...]] * reps, axis=1)
    pe = (pe * cos + _rotate_half_pairs(pe) * sin) * scale
    for h in range(hb):
        o_ref[h, :, :QK_NOPE] = nope[:, h * QK_NOPE:(h + 1) * QK_NOPE].astype(o_ref.dtype)
        o_ref[h, :, QK_NOPE:] = pe[:, h * QK_ROPE:(h + 1) * QK_ROPE].astype(o_ref.dtype)


def q_b(cq, w_q_b, cos_t, sin_t, hb=8, tm=1024):
    m, kd = cq.shape
    hd = QK_NOPE + QK_ROPE
    heads = w_q_b.shape[1] // hd
    hb = min(hb, heads)
    assert (hb * QK_ROPE) % LANES == 0
    w3 = w_q_b.reshape(kd, heads // hb, hb, hd)
    w = jnp.concatenate([w3[..., :QK_NOPE].reshape(kd, heads // hb, hb * QK_NOPE),
                         w3[..., QK_NOPE:].reshape(kd, heads // hb, hb * QK_ROPE)], axis=-1)
    gw = hb * hd
    w = w.reshape(kd, (heads // hb) * gw).astype(BF16)
    tm = _tile(m, tm)
    return pl.pallas_call(
        functools.partial(_q_b_kernel, hb=hb, scale=float(hd) ** -0.5),
        grid=(m // tm, heads // hb),
        in_specs=[pl.BlockSpec((tm, kd), lambda i, j: (i, 0)),
                  pl.BlockSpec((kd, gw), lambda i, j: (0, j)),
                  pl.BlockSpec((tm, LANES), lambda i, j: (i, 0)),
                  pl.BlockSpec((tm, LANES), lambda i, j: (i, 0))],
        out_specs=pl.BlockSpec((hb, tm, hd), lambda i, j: (j, i, 0)),
        out_shape=jax.ShapeDtypeStruct((heads, m, hd), BF16),
        compiler_params=_cparams("parallel", "parallel"),
        name="q_b",
    )(cq, w, cos_t, sin_t)


def _flash_kernel(q_ref, k_ref, v_ref, o_ref, m_ref, l_ref, acc_ref, *, tq, tk):
    qi = pl.program_id(2)
    q = q_ref[0]
    m_ref[...] = jnp.full_like(m_ref, -jnp.inf)
    l_ref[...] = jnp.zeros_like(l_ref)
    acc_ref[...] = jnp.zeros_like(acc_ref)
    contract_last = (((1,), (1,)), ((), ()))

    def step(j, masked):
        start = pl.multiple_of(j * tk, tk)
        k = k_ref[0, pl.ds(start, tk), :]
        v = v_ref[0, pl.ds(start, tk), :]
        s = lax.dot_general(q, k, contract_last, preferred_element_type=F32)
        if masked:
            row = qi * tq + lax.broadcasted_iota(jnp.int32, s.shape, 0)
            col = j * tk + lax.broadcasted_iota(jnp.int32, s.shape, 1)
            s = jnp.where(col <= row, s, -jnp.inf)
        m_prev = m_ref[...]
        m_new = jnp.maximum(m_prev, jnp.max(s, axis=1, keepdims=True))
        p = jnp.exp(s - m_new)
        alpha = jnp.exp(m_prev - m_new)
        l_ref[...] = alpha * l_ref[...] + jnp.sum(p, axis=1, keepdims=True)
        acc_ref[...] = alpha * acc_ref[...] + jnp.dot(p.astype(v.dtype), v, preferred_element_type=F32)
        m_ref[...] = m_new

    n_full = qi * (tq // tk)

    def body(j, carry):
        step(j, False)
        return carry

    lax.fori_loop(0, n_full, body, 0)
    for d in range(tq // tk):
        step(n_full + d, True)
    o_ref[...] = (acc_ref[...] / l_ref[...]).astype(o_ref.dtype)


def flash_attention(q, k, v, bsz, seq_len, tq=512, tk=512):
    heads, t, hd = q.shape
    tq = _tile(seq_len, tq)
    tk = _tile(tq, tk)
    nq = seq_len // tq
    kern = functools.partial(_flash_kernel, tq=tq, tk=tk)
    return pl.pallas_call(
        kern,
        grid=(bsz, heads, nq),
        in_specs=[pl.BlockSpec((1, tq, hd), lambda b, h, i: (h, b * nq + i, 0)),
                  pl.BlockSpec((1, seq_len, hd), lambda b, h, i: (h, b, 0)),
                  pl.BlockSpec((1, seq_len, V_DIM), lambda b, h, i: (h, b, 0))],
        out_specs=pl.BlockSpec((tq, V_DIM), lambda b, h, i: (b * nq + i, h)),
        out_shape=jax.ShapeDtypeStruct((t, heads * V_DIM), BF16),
        scratch_shapes=[pltpu.VMEM((tq, 1), F32), pltpu.VMEM((tq, 1), F32), pltpu.VMEM((tq, V_DIM), F32)],
        compiler_params=_cparams("parallel", "parallel", "arbitrary"),
        name="flash_attention",
    )(q, k, v)


def _pad_ffn(w_in, conv_w, conv_b, w_out, f_pad):
    f = conv_b.shape[0]
    pad = f_pad - f
    w_in_p = jnp.concatenate([jnp.pad(w_in[:, :f], ((0, 0), (0, pad))),
                              jnp.pad(w_in[:, f:], ((0, 0), (0, pad)))], axis=1).astype(BF16)
    return (w_in_p, jnp.pad(conv_w, ((0, 0), (0, pad))), jnp.pad(conv_b, (0, pad)),
            jnp.pad(w_out, ((0, pad), (0, 0))).astype(BF16))


def kernel(x, positions, ln_mix, ln_ffn, ln_final, ssm_A_re, ssm_A_im, ssm_log_dt, ssm_B_re, ssm_B_im,
           ssm_C_re, ssm_C_im, ssm_D, ssm_w_glu, ssm_b_glu, kv_in_norm, w_kv_a, kv_latent_norm, w_kv_b,
           w_q_a, q_latent_norm, w_q_b, w_o, ffn_w_in, ffn_conv_w, ffn_conv_b, ffn_w_out):
    bsz, seq_len, d = x.shape
    t = bsz * seq_len
    depth = ln_mix.shape[0]
    n_a = ssm_A_re.shape[0]
    f = ffn_conv_b.shape[1]
    f_pad = -(-f // 512) * 512 if f >= 512 else f

    xs = x.reshape(t, d).astype(F32)
    cos_t, sin_t = rope_tables(positions)
    k_heads = v_heads = None
    w_kv_b_bf = w_kv_b.astype(BF16)
    for l in range(depth):
        h = rmsnorm(xs, ln_mix[l], BF16)
        if l < n_a:
            y = s5_mixer_pre_glu(h, bsz, seq_len, ssm_A_re[l], ssm_A_im[l], ssm_log_dt[l], ssm_B_re[l],
                                 ssm_B_im[l], ssm_C_re[l], ssm_C_im[l], ssm_D[l])
            xs = mm_glu(y, ssm_w_glu[l].astype(BF16), ssm_b_glu[l], xs)
        else:
            b = l - n_a
            cq = q_a(h, w_q_a[b].astype(BF16), q_latent_norm[b])
            q_heads = q_b(cq, w_q_b[b], cos_t, sin_t)
            o = flash_attention(q_heads, k_heads, v_heads, bsz, seq_len)
            xs = mm_residual(o, w_o[b].astype(BF16), xs)
        w_in_p, conv_w_p, conv_b_p, w_out_p = _pad_ffn(ffn_w_in[l], ffn_conv_w[l], ffn_conv_b[l],
                                                       ffn_w_out[l], f_pad)
        act = ffn_in(rmsnorm(xs, ln_ffn[l], BF16), w_in_p, conv_w_p, conv_b_p, seq_len)
        xs = mm_residual(act, w_out_p, xs)
        if l == n_a - 1:
            ckv, krope = kv_a(rmsnorm(xs, kv_in_norm, BF16), w_kv_a, kv_latent_norm, cos_t, sin_t)
            k_heads, v_heads = kv_b(ckv, krope, w_kv_b_bf)
    return rmsnorm(xs, ln_final, x.dtype).reshape(bsz, seq_len, d)
```

```python
import functools
import math

import jax
import jax.numpy as jnp
from jax import lax
from jax.experimental import pallas as pl
from jax.experimental.pallas import tpu as pltpu

F32 = jnp.float32
BF16 = jnp.bfloat16

EPS = 1e-6
SSM_GROUP = 16
SSM_STATE = 64
MLA_HEADS = 64
KV_RANK = 512
QK_NOPE = 128
QK_ROPE = 64
V_DIM = 128
ROPE_THETA = 10000.0
CONV_W = 3

VMEM_LIMIT_BYTES = 60 * 1024 * 1024
LANES = 128
SSM_CHUNK = 64
STEPS_PER_LANE_TILE = LANES // SSM_GROUP


def _cparams(*sem):
    return pltpu.CompilerParams(dimension_semantics=sem, vmem_limit_bytes=VMEM_LIMIT_BYTES)


def _tile(dim, pref):
    t = min(dim, pref)
    while dim % t:
        t //= 2
    return t


def _rmsnorm_kernel(x_ref, g_ref, o_ref):
    x = x_ref[...].astype(F32)
    y = x * lax.rsqrt(jnp.mean(x * x, axis=-1, keepdims=True) + EPS)
    o_ref[...] = (y * g_ref[...]).astype(o_ref.dtype)


def rmsnorm(x, g, out_dtype):
    m, d = x.shape
    tm = _tile(m, 512)
    return pl.pallas_call(
        _rmsnorm_kernel,
        grid=(m // tm,),
        in_specs=[pl.BlockSpec((tm, d), lambda i: (i, 0)),
                  pl.BlockSpec((1, d), lambda i: (0, 0))],
        out_specs=pl.BlockSpec((tm, d), lambda i: (i, 0)),
        out_shape=jax.ShapeDtypeStruct((m, d), out_dtype),
        compiler_params=_cparams("parallel"),
        name="rmsnorm",
    )(x, g.reshape(1, d).astype(F32))


def _mm_res_kernel(a_ref, w_ref, r_ref, o_ref):
    o_ref[...] = r_ref[...] + jnp.dot(a_ref[...], w_ref[...], preferred_element_type=F32)


def mm_residual(a, w, layer, res, tm=512, tn=512):
    m, kd = a.shape
    n = w.shape[2]
    tm, tn = _tile(m, tm), _tile(n, tn)
    return pl.pallas_call(
        _mm_res_kernel,
        grid=(m // tm, n // tn),
        in_specs=[pl.BlockSpec((tm, kd), lambda i, j: (i, 0)),
                  pl.BlockSpec((None, kd, tn), lambda i, j: (layer, 0, j)),
                  pl.BlockSpec((tm, tn), lambda i, j: (i, j))],
        out_specs=pl.BlockSpec((tm, tn), lambda i, j: (i, j)),
        out_shape=jax.ShapeDtypeStruct((m, n), F32),
        compiler_params=_cparams("parallel", "parallel"),
        name="mm_residual",
    )(a, w, res)


def _mm_glu_kernel(a_ref, w1_ref, w2_ref, b1_ref, b2_ref, r_ref, o_ref):
    a = a_ref[...]
    z1 = jnp.dot(a, w1_ref[...], preferred_element_type=F32) + b1_ref[...]
    z2 = jnp.dot(a, w2_ref[...], preferred_element_type=F32) + b2_ref[...]
    o_ref[...] = r_ref[...] + z1 * jax.nn.sigmoid(z2)


def mm_glu(a, w, layer, b, res, tm=512, tn=512):
    m, kd = a.shape
    n = w.shape[2] // 2
    tm, tn = _tile(m, tm), _tile(n, tn)
    nj = n // tn
    b2d = b.reshape(1, 2 * n).astype(F32)
    return pl.pallas_call(
        _mm_glu_kernel,
        grid=(m // tm, nj),
        in_specs=[pl.BlockSpec((tm, kd), lambda i, j: (i, 0)),
                  pl.BlockSpec((None, kd, tn), lambda i, j: (layer, 0, j)),
                  pl.BlockSpec((None, kd, tn), lambda i, j: (layer, 0, j + nj)),
                  pl.BlockSpec((1, tn), lambda i, j: (0, j)),
                  pl.BlockSpec((1, tn), lambda i, j: (0, j + nj)),
                  pl.BlockSpec((tm, tn), lambda i, j: (i, j))],
        out_specs=pl.BlockSpec((tm, tn), lambda i, j: (i, j)),
        out_shape=jax.ShapeDtypeStruct((m, n), F32),
        compiler_params=_cparams("parallel", "parallel"),
        name="mm_glu",
    )(a, w, w, b2d, b2d, res)


def _ffn_in_kernel(a_ref, wg_ref, wu_ref, cw_ref, cb_ref, o_ref, wgb_ref, wub_ref, carry_ref, *,
                   tiles_per_seq, nsub):
    i = pl.program_id(1)

    @pl.when(i == 0)
    def _():
        wgb_ref[...] = wg_ref[...].astype(wgb_ref.dtype)
        wub_ref[...] = wu_ref[...].astype(wub_ref.dtype)

    @pl.when(i % tiles_per_seq == 0)
    def _():
        carry_ref[...] = jnp.zeros_like(carry_ref)

    cw = cw_ref[...]
    cb = cb_ref[...]
    rs = a_ref.shape[0] // nsub
    prev = carry_ref[...]
    for r in range(nsub):
        a = a_ref[r * rs:(r + 1) * rs, :]
        gate = jnp.dot(a, wgb_ref[...], preferred_element_type=F32)
        up = jnp.dot(a, wub_ref[...], preferred_element_type=F32)
        row = lax.broadcasted_iota(jnp.int32, gate.shape, 0)
        g1 = jnp.where(row == 0, prev[7:8, :], pltpu.roll(gate, 1, 0))
        g2 = jnp.where(row == 0, prev[6:7, :],
                       jnp.where(row == 1, prev[7:8, :], pltpu.roll(gate, 2, 0)))
        prev = gate[rs - 8:, :]
        conv = cw[0:1, :] * g2 + cw[1:2, :] * g1 + cw[2:3, :] * gate + cb
        o_ref[r * rs:(r + 1) * rs, :] = (jax.nn.silu(conv) * up).astype(o_ref.dtype)
    carry_ref[...] = prev


def ffn_in(a, w_in, layer, conv_w, conv_b, seq_len, tm=1024, tn=256, nsub=2):
    m, kd = a.shape
    f = w_in.shape[2] // 2
    tm, tn = _tile(min(m, seq_len), tm), _tile(f, tn)
    nsub = nsub if tm % nsub == 0 and tm // nsub >= 8 else 1
    assert seq_len % tm == 0 and tm >= 8
    nj = f // tn
    kern = functools.partial(_ffn_in_kernel, tiles_per_seq=seq_len // tm, nsub=nsub)
    return pl.pallas_call(
        kern,
        grid=(nj, m // tm),
        in_specs=[pl.BlockSpec((tm, kd), lambda j, i: (i, 0)),
                  pl.BlockSpec((None, kd, tn), lambda j, i: (layer, 0, j)),
                  pl.BlockSpec((None, kd, tn), lambda j, i: (layer, 0, j + nj)),
                  pl.BlockSpec((CONV_W, tn), lambda j, i: (0, j)),
                  pl.BlockSpec((1, tn), lambda j, i: (0, j))],
        out_specs=pl.BlockSpec((tm, tn), lambda j, i: (i, j)),
        out_shape=jax.ShapeDtypeStruct((m, f), BF16),
        scratch_shapes=[pltpu.VMEM((kd, tn), BF16), pltpu.VMEM((kd, tn), BF16), pltpu.VMEM((8, tn), F32)],
        compiler_params=_cparams("parallel", "arbitrary"),
        name="ffn_in",
    )(a, w_in, w_in, conv_w.astype(F32), conv_b.reshape(1, f).astype(F32))


def _cmul(ar, ai, br, bi):
    return ar * br - ai * bi, ar * bi + ai * br


def _s5_tables_kernel(acol_ref, arow_ref, ldt_ref, bt_re_ref, bt_im_ref, btr_re_ref, btr_im_ref,
                      ct_re_ref, ct_im_ref,
                      pt_re_ref, pt_im_ref, q_re_ref, q_im_ref, tblk_ref, laml_ref, *, chunk):
    nblk = chunk // STEPS_PER_LANE_TILE
    dt = jnp.exp(ldt_ref[0])
    ar, ai = acol_ref[0][:, 0:1], acol_ref[0][:, 1:2]
    adr, adi = ar * dt, ai * dt
    p = ar.shape[0]

    def cpow(e, xr, xi):
        mag = jnp.exp(e * xr)
        return mag * jnp.cos(e * xi), mag * jnp.sin(e * xi)

    lane = lax.broadcasted_iota(jnp.int32, (p, LANES), 1)
    kb = (lane // SSM_GROUP).astype(F32)
    eb_re, eb_im = cpow(kb, adr, adi)
    er_re, er_im = cpow((STEPS_PER_LANE_TILE - 1) - kb, adr, adi)
    lk_re, lk_im = cpow(lane.astype(F32), adr, adi)

    def zoh(lr, li, xr, xi):
        den = xr * xr + xi * xi
        return ((lr - 1.0) * xr + li * xi) / den, (li * xr - (lr - 1.0) * xi) / den

    f_re, f_im = zoh(lk_re[:, 1:2], lk_im[:, 1:2], ar, ai)
    bb_re, bb_im = _cmul(f_re, f_im, bt_re_ref[0], bt_im_ref[0])

    arr, air = arow_ref[0][0:1, :], arow_ref[0][1:2, :]
    one = jnp.ones_like(arr)
    lr_row, li_row = cpow(one, arr * dt, air * dt)
    fr_re, fr_im = zoh(lr_row, li_row, arr, air)
    bbt_re, bbt_im = _cmul(fr_re, fr_im, btr_re_ref[0], btr_im_ref[0])
    ll_re, ll_im = cpow(one * float(chunk), arr * dt, air * dt)
    laml_ref[0] = jnp.concatenate([ll_re, ll_im], axis=0)

    ct_re, ct_im = ct_re_ref[0], ct_im_ref[0]
    ktabs = [jnp.zeros((SSM_GROUP, LANES), F32)]
    for j in range(nblk):
        k0 = j * STEPS_PER_LANE_TILE
        kr = chunk - STEPS_PER_LANE_TILE - k0
        e0 = _cmul(eb_re, eb_im, lk_re[:, k0:k0 + 1], lk_im[:, k0:k0 + 1])
        e1 = _cmul(eb_re, eb_im, lk_re[:, k0 + 1:k0 + 2], lk_im[:, k0 + 1:k0 + 2])
        ev = _cmul(er_re, er_im, lk_re[:, kr:kr + 1], lk_im[:, kr:kr + 1])
        sl = slice(j * LANES, (j + 1) * LANES)
        qr, qi = _cmul(e1[0], e1[1], ct_re, ct_im)
        q_re_ref[0, :, sl] = qr.astype(q_re_ref.dtype)
        q_im_ref[0, :, sl] = (-qi).astype(q_im_ref.dtype)
        pr, pi = _cmul(ev[0], ev[1], bb_re, bb_im)
        pt_re_ref[0, :, sl] = pr.astype(pt_re_ref.dtype)
        pt_im_ref[0, :, sl] = pi.astype(pt_im_ref.dtype)
        wr, wi = _cmul(e0[0], e0[1], ct_re, ct_im)
        ktabs.append(jnp.dot(bbt_re, wr, preferred_element_type=F32, precision=lax.Precision.HIGHEST)
                     - jnp.dot(bbt_im, wi, preferred_element_type=F32, precision=lax.Precision.HIGHEST))
    ktab = jnp.concatenate(ktabs, axis=1)
    for s in range(STEPS_PER_LANE_TILE):
        shifted = pltpu.roll(ktab, SSM_GROUP * s, 1) if s else ktab
        for d in range(nblk):
            tblk_ref[0, d, s * SSM_GROUP:(s + 1) * SSM_GROUP, :] = (
                shifted[:, (d + 1) * LANES:(d + 2) * LANES].astype(tblk_ref.dtype))


def s5_tables(a_re, a_im, log_dt, b_re, b_im, c_re, c_im, chunk):
    g, p = a_re.shape
    c = b_re.shape[-1]
    assert c == SSM_GROUP and chunk % STEPS_PER_LANE_TILE == 0 and chunk < LANES
    w = chunk * c
    nblk = chunk // STEPS_PER_LANE_TILE
    rep = LANES // c
    acol = jnp.stack([a_re, a_im], axis=-1).astype(F32)
    arow = jnp.stack([a_re, a_im], axis=1).astype(F32)
    ldt = log_dt.reshape(g, 1, 1).astype(F32)
    bt_re = jnp.tile(b_re.astype(F32), (1, 1, rep))
    bt_im = jnp.tile(b_im.astype(F32), (1, 1, rep))
    btr_re = jnp.swapaxes(b_re, 1, 2).astype(F32)
    btr_im = jnp.swapaxes(b_im, 1, 2).astype(F32)
    ct_re = jnp.tile(jnp.swapaxes(c_re, 1, 2).astype(F32), (1, 1, rep))
    ct_im = jnp.tile(jnp.swapaxes(c_im, 1, 2).astype(F32), (1, 1, rep))
    g3 = lambda s1, s2: pl.BlockSpec((1, s1, s2), lambda i: (i, 0, 0))
    return pl.pallas_call(
        functools.partial(_s5_tables_kernel, chunk=chunk),
        grid=(g,),
        in_specs=[g3(p, 2), g3(2, p), g3(1, 1), g3(p, LANES), g3(p, LANES), g3(c, p), g3(c, p),
                  g3(p, LANES), g3(p, LANES)],
        out_specs=[g3(p, w), g3(p, w), g3(p, w), g3(p, w),
                   pl.BlockSpec((1, nblk, LANES, LANES), lambda i: (i, 0, 0, 0)),
                   g3(2, p)],
        out_shape=[jax.ShapeDtypeStruct((g, p, w), BF16)] * 4
        + [jax.ShapeDtypeStruct((g, nblk, LANES, LANES), BF16),
           jax.ShapeDtypeStruct((g, 2, p), F32)],
        compiler_params=_cparams("parallel"),
        name="s5_tables",
    )(acol, arow, ldt, bt_re, bt_im, btr_re, btr_im, ct_re, ct_im)


def _gelu_tanh(x):
    c = math.sqrt(2.0 / math.pi)
    return x * (0.5 * (1.0 + jnp.tanh(c * (x + 0.044715 * (x * x * x)))))


def _s5_scan_kernel(u_ref, pt_re_ref, pt_im_ref, q_re_ref, q_im_ref, tblk_ref, laml_ref, dsk_ref, o_ref,
                    m_ref, s_re_ref, s_im_ref, h_re_ref, h_im_ref, *, nblk, nchunks, bsz):
    @pl.when(pl.program_id(0) == 0)
    def _():
        m_ref[...] = jnp.zeros_like(m_ref)

    for i in range(nblk):
        for j in range(i, nblk):
            m_ref[i * LANES:(i + 1) * LANES, j * LANES:(j + 1) * LANES] = tblk_ref[0, j - i]

    u = u_ref[0]
    contract_last = (((1,), (1,)), ((), ()))
    s_re_ref[...] = lax.dot_general(u, pt_re_ref[0], contract_last, preferred_element_type=F32)
    s_im_ref[...] = lax.dot_general(u, pt_im_ref[0], contract_last, preferred_element_type=F32)

    lr, li = laml_ref[0][0:1, :], laml_ref[0][1:2, :]
    p = lr.shape[1]
    hr = jnp.zeros((bsz, p), F32)
    hi = jnp.zeros((bsz, p), F32)
    for c in range(nchunks):
        rows = slice(c * bsz, (c + 1) * bsz)
        h_re_ref[rows, :] = hr
        h_im_ref[rows, :] = hi
        hr, hi = (lr * hr - li * hi + s_re_ref[rows, :],
                  lr * hi + li * hr + s_im_ref[rows, :])

    y = jnp.dot(u, m_ref[...], preferred_element_type=F32)
    y += jnp.dot(h_re_ref[...].astype(BF16), q_re_ref[0], preferred_element_type=F32)
    y += jnp.dot(h_im_ref[...].astype(BF16), q_im_ref[0], preferred_element_type=F32)
    y += dsk_ref[0] * u.astype(F32)
    o_ref[0] = _gelu_tanh(y).astype(o_ref.dtype)


def s5_scan(u_g, tables, d_skip_t, bsz):
    pt_re, pt_im, q_re, q_im, tblk, laml = tables
    g, r, w = u_g.shape
    p = pt_re.shape[1]
    nblk = tblk.shape[1]
    g3 = lambda s1, s2: pl.BlockSpec((1, s1, s2), lambda i: (i, 0, 0))
    kern = functools.partial(_s5_scan_kernel, nblk=nblk, nchunks=r // bsz, bsz=bsz)
    return pl.pallas_call(
        kern,
        grid=(g,),
        in_specs=[g3(r, w), g3(p, w), g3(p, w), g3(p, w), g3(p, w),
                  pl.BlockSpec((1, nblk, LANES, LANES), lambda i: (i, 0, 0, 0)),
                  g3(2, p), g3(1, w)],
        out_specs=g3(r, w),
        out_shape=jax.ShapeDtypeStruct((g, r, w), BF16),
        scratch_shapes=[pltpu.VMEM((w, w), BF16)] + [pltpu.VMEM((r, p), F32)] * 4,
        compiler_params=_cparams("arbitrary"),
        name="s5_scan",
    )(u_g, pt_re, pt_im, q_re, q_im, tblk, laml, d_skip_t)


def s5_mixer_pre_glu(h, bsz, seq_len, a_re, a_im, log_dt, b_re, b_im, c_re, c_im, d_skip):
    t, d = h.shape
    g = d // SSM_GROUP
    chunk = min(SSM_CHUNK, seq_len)
    nc = seq_len // chunk
    tables = s5_tables(a_re, a_im, log_dt, b_re, b_im, c_re, c_im, chunk)
    u_g = (h.reshape(bsz, nc, chunk, g, SSM_GROUP).transpose(3, 1, 0, 2, 4)
           .reshape(g, nc * bsz, chunk * SSM_GROUP))
    d_t = jnp.tile(d_skip.reshape(g, 1, SSM_GROUP).astype(F32), (1, 1, chunk))
    y_g = s5_scan(u_g, tables, d_t, bsz)
    return (y_g.reshape(g, nc, bsz, chunk, SSM_GROUP).transpose(2, 1, 3, 0, 4).reshape(t, d))


def _rope_tables_kernel(pos_ref, freq_ref, sign_ref, cos_ref, sin_ref):
    ang = pos_ref[...] * freq_ref[...]
    cos_ref[...] = jnp.cos(ang)
    sin_ref[...] = jnp.sin(ang) * sign_ref[...]


def rope_tables(positions):
    t = positions.size
    half = QK_ROPE // 2
    inv_freq = ROPE_THETA ** (-jnp.arange(half, dtype=F32) / half)
    freq = jnp.tile(inv_freq, LANES // half).reshape(1, LANES)
    sign = jnp.tile(jnp.concatenate([-jnp.ones(half, F32), jnp.ones(half, F32)]),
                    LANES // QK_ROPE).reshape(1, LANES)
    tm = _tile(t, 2048)
    row = pl.BlockSpec((1, LANES), lambda i: (0, 0))
    return pl.pallas_call(
        _rope_tables_kernel,
        grid=(t // tm,),
        in_specs=[pl.BlockSpec((tm, 1), lambda i: (i, 0)), row, row],
        out_specs=[pl.BlockSpec((tm, LANES), lambda i: (i, 0))] * 2,
        out_shape=[jax.ShapeDtypeStruct((t, LANES), F32)] * 2,
        compiler_params=_cparams("parallel"),
        name="rope_tables",
    )(positions.reshape(t, 1).astype(F32), freq, sign)


def _rotate_half_pairs(x):
    n = x.shape[-1]
    lane = lax.broadcasted_iota(jnp.int32, x.shape, x.ndim - 1)
    half = QK_ROPE // 2
    return jnp.where(lane % QK_ROPE < half, pltpu.roll(x, n - half, x.ndim - 1),
                     pltpu.roll(x, half, x.ndim - 1))


def _kv_a_kernel(a_ref, w_ref, g_ref, cos_ref, sin_ref, ckv_ref, krope_ref):
    acc = jnp.dot(a_ref[...], w_ref[...], preferred_element_type=F32)
    c = acc[:, :KV_RANK]
    c = c * lax.rsqrt(jnp.mean(c * c, axis=-1, keepdims=True) + EPS)
    ckv_ref[...] = (c * g_ref[...]).astype(ckv_ref.dtype)
    r = acc[:, KV_RANK:]
    r = r * cos_ref[...] + _rotate_half_pairs(r) * sin_ref[...]
    krope_ref[...] = r[:, :QK_ROPE].astype(krope_ref.dtype)


def kv_a(a, w_kv_a, g_kv, cos_t, sin_t, tm=512):
    m, kd = a.shape
    n = KV_RANK + LANES
    w = jnp.pad(w_kv_a, ((0, 0), (0, n - w_kv_a.shape[1]))).astype(BF16)
    tm = _tile(m, tm)
    return pl.pallas_call(
        _kv_a_kernel,
        grid=(m // tm,),
        in_specs=[pl.BlockSpec((tm, kd), lambda i: (i, 0)),
                  pl.BlockSpec((kd, n), lambda i: (0, 0)),
                  pl.BlockSpec((1, KV_RANK), lambda i: (0, 0)),
                  pl.BlockSpec((tm, LANES), lambda i: (i, 0)),
                  pl.BlockSpec((tm, LANES), lambda i: (i, 0))],
        out_specs=[pl.BlockSpec((tm, KV_RANK), lambda i: (i, 0)),
                   pl.BlockSpec((tm, QK_ROPE), lambda i: (i, 0))],
        out_shape=[jax.ShapeDtypeStruct((m, KV_RANK), BF16),
                   jax.ShapeDtypeStruct((m, QK_ROPE), BF16)],
        compiler_params=_cparams("parallel"),
        name="kv_a",
    )(a, w, g_kv.reshape(1, KV_RANK).astype(F32), cos_t, sin_t)


def _kv_b_kernel(c_ref, kr_ref, w_ref, k_ref, v_ref, *, hb):
    acc = jnp.dot(c_ref[...], w_ref[...], preferred_element_type=F32)
    kr = kr_ref[...]
    hw = QK_NOPE + V_DIM
    for h in range(hb):
        k_ref[h, :, :QK_NOPE] = acc[:, h * hw:h * hw + QK_NOPE].astype(k_ref.dtype)
        k_ref[h, :, QK_NOPE:] = kr
        v_ref[h] = acc[:, h * hw + QK_NOPE:(h + 1) * hw].T.astype(v_ref.dtype)


def kv_b(ckv, krope, w_kv_b, hb=8, tm=1024):
    m = ckv.shape[0]
    hw = QK_NOPE + V_DIM
    heads = w_kv_b.shape[1] // hw
    hb = min(hb, heads)
    tm = _tile(m, tm)
    return pl.pallas_call(
        functools.partial(_kv_b_kernel, hb=hb),
        grid=(m // tm, heads // hb),
        in_specs=[pl.BlockSpec((tm, KV_RANK), lambda i, j: (i, 0)),
                  pl.BlockSpec((tm, QK_ROPE), lambda i, j: (i, 0)),
                  pl.BlockSpec((KV_RANK, hb * hw), lambda i, j: (0, j))],
        out_specs=[pl.BlockSpec((hb, tm, QK_NOPE + QK_ROPE), lambda i, j: (j, i, 0)),
                   pl.BlockSpec((hb, V_DIM, tm), lambda i, j: (j, 0, i))],
        out_shape=[jax.ShapeDtypeStruct((heads, m, QK_NOPE + QK_ROPE), BF16),
                   jax.ShapeDtypeStruct((heads, V_DIM, m), BF16)],
        compiler_params=_cparams("parallel", "parallel"),
        name="kv_b",
    )(ckv, krope, w_kv_b)


def _q_a_kernel(a_ref, w_ref, g_ref, o_ref):
    c = jnp.dot(a_ref[...], w_ref[...], preferred_element_type=F32)
    c = c * lax.rsqrt(jnp.mean(c * c, axis=-1, keepdims=True) + EPS)
    o_ref[...] = (c * g_ref[...]).astype(o_ref.dtype)


def q_a(a, w, layer, g, tm=512):
    m, kd = a.shape
    n = w.shape[2]
    tm = _tile(m, tm)
    return pl.pallas_call(
        _q_a_kernel,
        grid=(m // tm,),
        in_specs=[pl.BlockSpec((tm, kd), lambda i: (i, 0)),
                  pl.BlockSpec((None, kd, n), lambda i: (layer, 0, 0)),
                  pl.BlockSpec((1, n), lambda i: (0, 0))],
        out_specs=pl.BlockSpec((tm, n), lambda i: (i, 0)),
        out_shape=jax.ShapeDtypeStruct((m, n), BF16),
        compiler_params=_cparams("parallel"),
        name="q_a",
    )(a, w, g.reshape(1, n).astype(F32))


def _q_b_kernel(c_ref, w_ref, cos_ref, sin_ref, o_ref, *, hb, scale):
    acc = jnp.dot(c_ref[...], w_ref[...], preferred_element_type=F32)
    nope = acc[:, :hb * QK_NOPE] * scale
    pe = acc[:, hb * QK_NOPE:]
    reps = hb * QK_ROPE // LANES
    cos = jnp.concatenate([cos_ref[...]] * reps, axis=1)
    sin = jnp.concatenate([sin_ref[...]] * reps, axis=1)
    pe = (pe * cos + _rotate_half_pairs(pe) * sin) * scale
    for h in range(hb):
        o_ref[h, :, :QK_NOPE] = nope[:, h * QK_NOPE:(h + 1) * QK_NOPE].astype(o_ref.dtype)
        o_ref[h, :, QK_NOPE:] = pe[:, h * QK_ROPE:(h + 1) * QK_ROPE].astype(o_ref.dtype)


def q_b(cq, w_q_b, cos_t, sin_t, hb=8, tm=1024):
    m, kd = cq.shape
    hd = QK_NOPE + QK_ROPE
    heads = w_q_b.shape[1] // hd
    hb = min(hb, heads)
    assert (hb * QK_ROPE) % LANES == 0
    w3 = w_q_b.reshape(kd, heads // hb, hb, hd)
    w = jnp.concatenate([w3[..., :QK_NOPE].reshape(kd, heads // hb, hb * QK_NOPE),
                         w3[..., QK_NOPE:].reshape(kd, heads // hb, hb * QK_ROPE)], axis=-1)
    gw = hb * hd
    w = w.reshape(kd, (heads // hb) * gw).astype(BF16)
    tm = _tile(m, tm)
    return pl.pallas_call(
        functools.partial(_q_b_kernel, hb=hb, scale=float(hd) ** -0.5 * math.log2(math.e)),
        grid=(m // tm, heads // hb),
        in_specs=[pl.BlockSpec((tm, kd), lambda i, j: (i, 0)),
                  pl.BlockSpec((kd, gw), lambda i, j: (0, j)),
                  pl.BlockSpec((tm, LANES), lambda i, j: (i, 0)),
                  pl.BlockSpec((tm, LANES), lambda i, j: (i, 0))],
        out_specs=pl.BlockSpec((hb, tm, hd), lambda i, j: (j, i, 0)),
        out_shape=jax.ShapeDtypeStruct((heads, m, hd), BF16),
        compiler_params=_cparams("parallel", "parallel"),
        name="q_b",
    )(cq, w, cos_t, sin_t)


def _flash_kernel(q_ref, k_ref, vt_ref, o_ref, s0_ref, s1_ref, m_ref, l_ref, acc_ref, *, tq):
    qi = pl.program_id(2)
    q = q_ref[0]
    m_ref[...] = jnp.full_like(m_ref, -jnp.inf)
    l_ref[...] = jnp.zeros_like(l_ref)
    acc_ref[...] = jnp.zeros_like(acc_ref)
    contract_last = (((1,), (1,)), ((), ()))

    def scores(j, s_ref):
        k = k_ref[0, pl.ds(pl.multiple_of(j * tq, tq), tq), :]
        s_ref[...] = lax.dot_general(k, q, contract_last, preferred_element_type=F32)

    def update(j, s_ref, masked):
        vt = vt_ref[0, :, pl.ds(pl.multiple_of(j * tq, tq), tq)]
        s = s_ref[...]
        if masked:
            key = lax.broadcasted_iota(jnp.int32, s.shape, 0)
            qry = lax.broadcasted_iota(jnp.int32, s.shape, 1)
            s = jnp.where(key <= qry, s, -jnp.inf)
        m_prev = m_ref[...]
        m_new = jnp.maximum(m_prev, jnp.max(s, axis=0, keepdims=True))
        p = jnp.exp2(s - m_new)
        alpha = jnp.exp2(m_prev - m_new)
        l_ref[...] = alpha * l_ref[...] + jnp.sum(p, axis=0, keepdims=True)
        acc_ref[...] = alpha * acc_ref[...] + jnp.dot(vt, p.astype(vt.dtype), preferred_element_type=F32)
        m_ref[...] = m_new

    scores(0, s0_ref)

    def pair(p, carry):
        scores(2 * p + 1, s1_ref)
        update(2 * p, s0_ref, False)
        scores(2 * p + 2, s0_ref)
        update(2 * p + 1, s1_ref, False)
        return carry

    lax.fori_loop(0, qi // 2, pair, 0)

    @pl.when(qi % 2 == 1)
    def _():
        scores(qi, s1_ref)
        update(qi - 1, s0_ref, False)
        update(qi, s1_ref, True)

    @pl.when(qi % 2 == 0)
    def _():
        update(qi, s0_ref, True)

    o_ref[...] = (acc_ref[...] / l_ref[...]).T.astype(o_ref.dtype)


def flash_attention(q, k, vt, bsz, seq_len, tq=512):
    heads, t, hd = q.shape
    tq = _tile(seq_len, tq)
    nq = seq_len // tq
    kern = functools.partial(_flash_kernel, tq=tq)
    return pl.pallas_call(
        kern,
        grid=(bsz, heads, nq),
        in_specs=[pl.BlockSpec((1, tq, hd), lambda b, h, i: (h, b * nq + i, 0)),
                  pl.BlockSpec((1, seq_len, hd), lambda b, h, i: (h, b, 0)),
                  pl.BlockSpec((1, V_DIM, seq_len), lambda b, h, i: (h, 0, b))],
        out_specs=pl.BlockSpec((tq, V_DIM), lambda b, h, i: (b * nq + i, h)),
        out_shape=jax.ShapeDtypeStruct((t, heads * V_DIM), BF16),
        scratch_shapes=[pltpu.VMEM((tq, tq), F32), pltpu.VMEM((tq, tq), F32),
                        pltpu.VMEM((1, tq), F32), pltpu.VMEM((1, tq), F32),
                        pltpu.VMEM((V_DIM, tq), F32)],
        compiler_params=_cparams("parallel", "parallel", "arbitrary"),
        name="flash_attention",
    )(q, k, vt)


def kernel(x, positions, ln_mix, ln_ffn, ln_final, ssm_A_re, ssm_A_im, ssm_log_dt, ssm_B_re, ssm_B_im,
           ssm_C_re, ssm_C_im, ssm_D, ssm_w_glu, ssm_b_glu, kv_in_norm, w_kv_a, kv_latent_norm, w_kv_b,
           w_q_a, q_latent_norm, w_q_b, w_o, ffn_w_in, ffn_conv_w, ffn_conv_b, ffn_w_out):
    bsz, seq_len, d = x.shape
    t = bsz * seq_len
    depth = ln_mix.shape[0]
    n_a = ssm_A_re.shape[0]

    xs = x.reshape(t, d).astype(F32)
    cos_t, sin_t = rope_tables(positions)
    k_heads = v_heads = None
    w_kv_b_bf = w_kv_b.astype(BF16)
    w_glu_bf, w_q_a_bf, w_o_bf, w_out_bf = (w.astype(BF16) for w in (ssm_w_glu, w_q_a, w_o, ffn_w_out))
    for l in range(depth):
        h = rmsnorm(xs, ln_mix[l], BF16)
        if l < n_a:
            y = s5_mixer_pre_glu(h, bsz, seq_len, ssm_A_re[l], ssm_A_im[l], ssm_log_dt[l], ssm_B_re[l],
                                 ssm_B_im[l], ssm_C_re[l], ssm_C_im[l], ssm_D[l])
            xs = mm_glu(y, w_glu_bf, l, ssm_b_glu[l], xs)
        else:
            b = l - n_a
            cq = q_a(h, w_q_a_bf, b, q_latent_norm[b])
            q_heads = q_b(cq, w_q_b[b], cos_t, sin_t)
            o = flash_attention(q_heads, k_heads, v_heads, bsz, seq_len)
            xs = mm_residual(o, w_o_bf, b, xs)
        act = ffn_in(rmsnorm(xs, ln_ffn[l], BF16), ffn_w_in, l, ffn_conv_w[l], ffn_conv_b[l], seq_len)
        xs = mm_residual(act, w_out_bf, l, xs)
        if l == n_a - 1:
            ckv, krope = kv_a(rmsnorm(xs, kv_in_norm, BF16), w_kv_a, kv_latent_norm, cos_t, sin_t)
            k_heads, v_heads = kv_b(ckv, krope, w_kv_b_bf)
    return rmsnorm(xs, ln_final, x.dtype).reshape(bsz, seq_len, d)
```

```python
import functools
import math

import jax
import jax.numpy as jnp
from jax import lax
from jax.experimental import pallas as pl
from jax.experimental.pallas import tpu as pltpu

F32 = jnp.float32
BF16 = jnp.bfloat16

EPS = 1e-6
SSM_GROUP = 16
SSM_STATE = 64
MLA_HEADS = 64
KV_RANK = 512
QK_NOPE = 128
QK_ROPE = 64
V_DIM = 128
ROPE_THETA = 10000.0
CONV_W = 3

VMEM_LIMIT_BYTES = 60 * 1024 * 1024
LANES = 128
SSM_CHUNK = 64
STEPS_PER_LANE_TILE = LANES // SSM_GROUP


def _cparams(*sem):
    return pltpu.CompilerParams(dimension_semantics=sem, vmem_limit_bytes=VMEM_LIMIT_BYTES)


def _tile(dim, pref):
    t = min(dim, pref)
    while dim % t:
        t //= 2
    return t


def _rmsnorm_kernel(x_ref, g_ref, o_ref):
    x = x_ref[...].astype(F32)
    y = x * lax.rsqrt(jnp.mean(x * x, axis=-1, keepdims=True) + EPS)
    o_ref[...] = (y * g_ref[...]).astype(o_ref.dtype)


def rmsnorm(x, g, out_dtype):
    m, d = x.shape
    tm = _tile(m, 512)
    return pl.pallas_call(
        _rmsnorm_kernel,
        grid=(m // tm,),
        in_specs=[pl.BlockSpec((tm, d), lambda i: (i, 0)),
                  pl.BlockSpec((1, d), lambda i: (0, 0))],
        out_specs=pl.BlockSpec((tm, d), lambda i: (i, 0)),
        out_shape=jax.ShapeDtypeStruct((m, d), out_dtype),
        compiler_params=_cparams("parallel"),
        name="rmsnorm",
    )(x, g.reshape(1, d).astype(F32))


def _mm_res_kernel(a_ref, w_ref, r_ref, o_ref):
    o_ref[...] = r_ref[...] + jnp.dot(a_ref[...], w_ref[...], preferred_element_type=F32)


def mm_residual(a, w, layer, res, tm=512, tn=512):
    m, kd = a.shape
    n = w.shape[2]
    tm, tn = _tile(m, tm), _tile(n, tn)
    return pl.pallas_call(
        _mm_res_kernel,
        grid=(m // tm, n // tn),
        in_specs=[pl.BlockSpec((tm, kd), lambda i, j: (i, 0)),
                  pl.BlockSpec((None, kd, tn), lambda i, j: (layer, 0, j)),
                  pl.BlockSpec((tm, tn), lambda i, j: (i, j))],
        out_specs=pl.BlockSpec((tm, tn), lambda i, j: (i, j)),
        out_shape=jax.ShapeDtypeStruct((m, n), F32),
        compiler_params=_cparams("parallel", "parallel"),
        name="mm_residual",
    )(a, w, res)


def _mm_glu_kernel(a_ref, w1_ref, w2_ref, b1_ref, b2_ref, r_ref, o_ref):
    a = a_ref[...]
    z1 = jnp.dot(a, w1_ref[...], preferred_element_type=F32) + b1_ref[...]
    z2 = jnp.dot(a, w2_ref[...], preferred_element_type=F32) + b2_ref[...]
    o_ref[...] = r_ref[...] + z1 * jax.nn.sigmoid(z2)


def mm_glu(a, w, layer, b, res, tm=512, tn=512):
    m, kd = a.shape
    n = w.shape[2] // 2
    tm, tn = _tile(m, tm), _tile(n, tn)
    nj = n // tn
    b2d = b.reshape(1, 2 * n).astype(F32)
    return pl.pallas_call(
        _mm_glu_kernel,
        grid=(m // tm, nj),
        in_specs=[pl.BlockSpec((tm, kd), lambda i, j: (i, 0)),
                  pl.BlockSpec((None, kd, tn), lambda i, j: (layer, 0, j)),
                  pl.BlockSpec((None, kd, tn), lambda i, j: (layer, 0, j + nj)),
                  pl.BlockSpec((1, tn), lambda i, j: (0, j)),
                  pl.BlockSpec((1, tn), lambda i, j: (0, j + nj)),
                  pl.BlockSpec((tm, tn), lambda i, j: (i, j))],
        out_specs=pl.BlockSpec((tm, tn), lambda i, j: (i, j)),
        out_shape=jax.ShapeDtypeStruct((m, n), F32),
        compiler_params=_cparams("parallel", "parallel"),
        name="mm_glu",
    )(a, w, w, b2d, b2d, res)


def _ffn_in_kernel(a_ref, wg_ref, wu_ref, cw_ref, cb_ref, o_ref, wgb_ref, wub_ref, carry_ref, *,
                   tiles_per_seq, nsub):
    i = pl.program_id(1)

    @pl.when(i == 0)
    def _():
        wgb_ref[...] = wg_ref[...].astype(wgb_ref.dtype)
        wub_ref[...] = wu_ref[...].astype(wub_ref.dtype)

    @pl.when(i % tiles_per_seq == 0)
    def _():
        carry_ref[...] = jnp.zeros_like(carry_ref)

    cw = cw_ref[...]
    cb = cb_ref[...]
    rs = a_ref.shape[0] // nsub
    prev = carry_ref[...]
    for r in range(nsub):
        a = a_ref[r * rs:(r + 1) * rs, :]
        gate = jnp.dot(a, wgb_ref[...], preferred_element_type=F32)
        up = jnp.dot(a, wub_ref[...], preferred_element_type=F32)
        row = lax.broadcasted_iota(jnp.int32, gate.shape, 0)
        g1 = jnp.where(row == 0, prev[7:8, :], pltpu.roll(gate, 1, 0))
        g2 = jnp.where(row == 0, prev[6:7, :],
                       jnp.where(row == 1, prev[7:8, :], pltpu.roll(gate, 2, 0)))
        prev = gate[rs - 8:, :]
        conv = cw[0:1, :] * g2 + cw[1:2, :] * g1 + cw[2:3, :] * gate + cb
        o_ref[r * rs:(r + 1) * rs, :] = (jax.nn.silu(conv) * up).astype(o_ref.dtype)
    carry_ref[...] = prev


def ffn_in(a, w_in, layer, conv_w, conv_b, seq_len, tm=2048, tn=256, nsub=4):
    m, kd = a.shape
    f = w_in.shape[2] // 2
    tm, tn = _tile(min(m, seq_len), tm), _tile(f, tn)
    nsub = nsub if tm % nsub == 0 and tm // nsub >= 8 else 1
    assert seq_len % tm == 0 and tm >= 8
    nj = f // tn
    kern = functools.partial(_ffn_in_kernel, tiles_per_seq=seq_len // tm, nsub=nsub)
    return pl.pallas_call(
        kern,
        grid=(nj, m // tm),
        in_specs=[pl.BlockSpec((tm, kd), lambda j, i: (i, 0)),
                  pl.BlockSpec((None, kd, tn), lambda j, i: (layer, 0, j)),
                  pl.BlockSpec((None, kd, tn), lambda j, i: (layer, 0, j + nj)),
                  pl.BlockSpec((CONV_W, tn), lambda j, i: (0, j)),
                  pl.BlockSpec((1, tn), lambda j, i: (0, j))],
        out_specs=pl.BlockSpec((tm, tn), lambda j, i: (i, j)),
        out_shape=jax.ShapeDtypeStruct((m, f), BF16),
        scratch_shapes=[pltpu.VMEM((kd, tn), BF16), pltpu.VMEM((kd, tn), BF16), pltpu.VMEM((8, tn), F32)],
        compiler_params=_cparams("parallel", "arbitrary"),
        name="ffn_in",
    )(a, w_in, w_in, conv_w.astype(F32), conv_b.reshape(1, f).astype(F32))


def _cmul(ar, ai, br, bi):
    return ar * br - ai * bi, ar * bi + ai * br


def _s5_tables_kernel(acol_ref, arow_ref, ldt_ref, bt_re_ref, bt_im_ref, btr_re_ref, btr_im_ref,
                      ct_re_ref, ct_im_ref,
                      pt_re_ref, pt_im_ref, q_re_ref, q_im_ref, tblk_ref, laml_ref, *, chunk):
    nblk = chunk // STEPS_PER_LANE_TILE
    dt = jnp.exp(ldt_ref[0])
    ar, ai = acol_ref[0][:, 0:1], acol_ref[0][:, 1:2]
    adr, adi = ar * dt, ai * dt
    p = ar.shape[0]

    def cpow(e, xr, xi):
        mag = jnp.exp(e * xr)
        return mag * jnp.cos(e * xi), mag * jnp.sin(e * xi)

    lane = lax.broadcasted_iota(jnp.int32, (p, LANES), 1)
    kb = (lane // SSM_GROUP).astype(F32)
    eb_re, eb_im = cpow(kb, adr, adi)
    er_re, er_im = cpow((STEPS_PER_LANE_TILE - 1) - kb, adr, adi)
    lk_re, lk_im = cpow(lane.astype(F32), adr, adi)

    def zoh(lr, li, xr, xi):
        den = xr * xr + xi * xi
        return ((lr - 1.0) * xr + li * xi) / den, (li * xr - (lr - 1.0) * xi) / den

    f_re, f_im = zoh(lk_re[:, 1:2], lk_im[:, 1:2], ar, ai)
    bb_re, bb_im = _cmul(f_re, f_im, bt_re_ref[0], bt_im_ref[0])

    arr, air = arow_ref[0][0:1, :], arow_ref[0][1:2, :]
    one = jnp.ones_like(arr)
    lr_row, li_row = cpow(one, arr * dt, air * dt)
    fr_re, fr_im = zoh(lr_row, li_row, arr, air)
    bbt_re, bbt_im = _cmul(fr_re, fr_im, btr_re_ref[0], btr_im_ref[0])
    ll_re, ll_im = cpow(one * float(chunk), arr * dt, air * dt)
    laml_ref[0] = jnp.concatenate([ll_re, ll_im], axis=0)

    ct_re, ct_im = ct_re_ref[0], ct_im_ref[0]
    ktabs = [jnp.zeros((SSM_GROUP, LANES), F32)]
    for j in range(nblk):
        k0 = j * STEPS_PER_LANE_TILE
        kr = chunk - STEPS_PER_LANE_TILE - k0
        e0 = _cmul(eb_re, eb_im, lk_re[:, k0:k0 + 1], lk_im[:, k0:k0 + 1])
        e1 = _cmul(eb_re, eb_im, lk_re[:, k0 + 1:k0 + 2], lk_im[:, k0 + 1:k0 + 2])
        ev = _cmul(er_re, er_im, lk_re[:, kr:kr + 1], lk_im[:, kr:kr + 1])
        sl = slice(j * LANES, (j + 1) * LANES)
        qr, qi = _cmul(e1[0], e1[1], ct_re, ct_im)
        q_re_ref[0, :, sl] = qr.astype(q_re_ref.dtype)
        q_im_ref[0, :, sl] = (-qi).astype(q_im_ref.dtype)
        pr, pi = _cmul(ev[0], ev[1], bb_re, bb_im)
        pt_re_ref[0, :, sl] = pr.astype(pt_re_ref.dtype)
        pt_im_ref[0, :, sl] = pi.astype(pt_im_ref.dtype)
        wr, wi = _cmul(e0[0], e0[1], ct_re, ct_im)
        ktabs.append(jnp.dot(bbt_re, wr, preferred_element_type=F32, precision=lax.Precision.HIGHEST)
                     - jnp.dot(bbt_im, wi, preferred_element_type=F32, precision=lax.Precision.HIGHEST))
    ktab = jnp.concatenate(ktabs, axis=1)
    for s in range(STEPS_PER_LANE_TILE):
        shifted = pltpu.roll(ktab, SSM_GROUP * s, 1) if s else ktab
        for d in range(nblk):
            tblk_ref[0, d, s * SSM_GROUP:(s + 1) * SSM_GROUP, :] = (
                shifted[:, (d + 1) * LANES:(d + 2) * LANES].astype(tblk_ref.dtype))


def s5_tables(a_re, a_im, log_dt, b_re, b_im, c_re, c_im, chunk):
    g, p = a_re.shape
    c = b_re.shape[-1]
    assert c == SSM_GROUP and chunk % STEPS_PER_LANE_TILE == 0 and chunk < LANES
    w = chunk * c
    nblk = chunk // STEPS_PER_LANE_TILE
    rep = LANES // c
    acol = jnp.stack([a_re, a_im], axis=-1).astype(F32)
    arow = jnp.stack([a_re, a_im], axis=1).astype(F32)
    ldt = log_dt.reshape(g, 1, 1).astype(F32)
    bt_re = jnp.tile(b_re.astype(F32), (1, 1, rep))
    bt_im = jnp.tile(b_im.astype(F32), (1, 1, rep))
    btr_re = jnp.swapaxes(b_re, 1, 2).astype(F32)
    btr_im = jnp.swapaxes(b_im, 1, 2).astype(F32)
    ct_re = jnp.tile(jnp.swapaxes(c_re, 1, 2).astype(F32), (1, 1, rep))
    ct_im = jnp.tile(jnp.swapaxes(c_im, 1, 2).astype(F32), (1, 1, rep))
    g3 = lambda s1, s2: pl.BlockSpec((1, s1, s2), lambda i: (i, 0, 0))
    return pl.pallas_call(
        functools.partial(_s5_tables_kernel, chunk=chunk),
        grid=(g,),
        in_specs=[g3(p, 2), g3(2, p), g3(1, 1), g3(p, LANES), g3(p, LANES), g3(c, p), g3(c, p),
                  g3(p, LANES), g3(p, LANES)],
        out_specs=[g3(p, w), g3(p, w), g3(p, w), g3(p, w),
                   pl.BlockSpec((1, nblk, LANES, LANES), lambda i: (i, 0, 0, 0)),
                   g3(2, p)],
        out_shape=[jax.ShapeDtypeStruct((g, p, w), BF16)] * 4
        + [jax.ShapeDtypeStruct((g, nblk, LANES, LANES), BF16),
           jax.ShapeDtypeStruct((g, 2, p), F32)],
        compiler_params=_cparams("parallel"),
        name="s5_tables",
    )(acol, arow, ldt, bt_re, bt_im, btr_re, btr_im, ct_re, ct_im)


def _gelu_tanh(x):
    c = math.sqrt(2.0 / math.pi)
    return x * (0.5 * (1.0 + jnp.tanh(c * (x + 0.044715 * (x * x * x)))))


def _s5_scan_kernel(u_ref, pt_re_ref, pt_im_ref, q_re_ref, q_im_ref, tblk_ref, laml_ref, dsk_ref, o_ref,
                    m_ref, s_re_ref, s_im_ref, h_re_ref, h_im_ref, *, nblk, nchunks, bsz):
    @pl.when(pl.program_id(0) == 0)
    def _():
        m_ref[...] = jnp.zeros_like(m_ref)

    for i in range(nblk):
        for j in range(i, nblk):
            m_ref[i * LANES:(i + 1) * LANES, j * LANES:(j + 1) * LANES] = tblk_ref[0, j - i]

    u = u_ref[0]
    contract_last = (((1,), (1,)), ((), ()))
    s_re_ref[...] = lax.dot_general(u, pt_re_ref[0], contract_last, preferred_element_type=F32)
    s_im_ref[...] = lax.dot_general(u, pt_im_ref[0], contract_last, preferred_element_type=F32)

    lr, li = laml_ref[0][0:1, :], laml_ref[0][1:2, :]
    p = lr.shape[1]
    hr = jnp.zeros((bsz, p), F32)
    hi = jnp.zeros((bsz, p), F32)
    for c in range(nchunks):
        rows = slice(c * bsz, (c + 1) * bsz)
        h_re_ref[rows, :] = hr
        h_im_ref[rows, :] = hi
        hr, hi = (lr * hr - li * hi + s_re_ref[rows, :],
                  lr * hi + li * hr + s_im_ref[rows, :])

    y = jnp.dot(u, m_ref[...], preferred_element_type=F32)
    y += jnp.dot(h_re_ref[...].astype(BF16), q_re_ref[0], preferred_element_type=F32)
    y += jnp.dot(h_im_ref[...].astype(BF16), q_im_ref[0], preferred_element_type=F32)
    y += dsk_ref[0] * u.astype(F32)
    o_ref[0] = _gelu_tanh(y).astype(o_ref.dtype)


def s5_scan(u_g, tables, d_skip_t, bsz):
    pt_re, pt_im, q_re, q_im, tblk, laml = tables
    g, r, w = u_g.shape
    p = pt_re.shape[1]
    nblk = tblk.shape[1]
    g3 = lambda s1, s2: pl.BlockSpec((1, s1, s2), lambda i: (i, 0, 0))
    kern = functools.partial(_s5_scan_kernel, nblk=nblk, nchunks=r // bsz, bsz=bsz)
    return pl.pallas_call(
        kern,
        grid=(g,),
        in_specs=[g3(r, w), g3(p, w), g3(p, w), g3(p, w), g3(p, w),
                  pl.BlockSpec((1, nblk, LANES, LANES), lambda i: (i, 0, 0, 0)),
                  g3(2, p), g3(1, w)],
        out_specs=g3(r, w),
        out_shape=jax.ShapeDtypeStruct((g, r, w), BF16),
        scratch_shapes=[pltpu.VMEM((w, w), BF16)] + [pltpu.VMEM((r, p), F32)] * 4,
        compiler_params=_cparams("arbitrary"),
        name="s5_scan",
    )(u_g, pt_re, pt_im, q_re, q_im, tblk, laml, d_skip_t)


def s5_mixer_pre_glu(h, bsz, seq_len, a_re, a_im, log_dt, b_re, b_im, c_re, c_im, d_skip):
    t, d = h.shape
    g = d // SSM_GROUP
    chunk = min(SSM_CHUNK, seq_len)
    nc = seq_len // chunk
    tables = s5_tables(a_re, a_im, log_dt, b_re, b_im, c_re, c_im, chunk)
    u_g = (h.reshape(bsz, nc, chunk, g, SSM_GROUP).transpose(3, 1, 0, 2, 4)
           .reshape(g, nc * bsz, chunk * SSM_GROUP))
    d_t = jnp.tile(d_skip.reshape(g, 1, SSM_GROUP).astype(F32), (1, 1, chunk))
    y_g = s5_scan(u_g, tables, d_t, bsz)
    return (y_g.reshape(g, nc, bsz, chunk, SSM_GROUP).transpose(2, 1, 3, 0, 4).reshape(t, d))


def _rope_tables_kernel(pos_ref, freq_ref, sign_ref, cos_ref, sin_ref):
    ang = pos_ref[...] * freq_ref[...]
    cos_ref[...] = jnp.cos(ang)
    sin_ref[...] = jnp.sin(ang) * sign_ref[...]


def rope_tables(positions):
    t = positions.size
    half = QK_ROPE // 2
    inv_freq = ROPE_THETA ** (-jnp.arange(half, dtype=F32) / half)
    freq = jnp.tile(inv_freq, LANES // half).reshape(1, LANES)
    sign = jnp.tile(jnp.concatenate([-jnp.ones(half, F32), jnp.ones(half, F32)]),
                    LANES // QK_ROPE).reshape(1, LANES)
    tm = _tile(t, 2048)
    row = pl.BlockSpec((1, LANES), lambda i: (0, 0))
    return pl.pallas_call(
        _rope_tables_kernel,
        grid=(t // tm,),
        in_specs=[pl.BlockSpec((tm, 1), lambda i: (i, 0)), row, row],
        out_specs=[pl.BlockSpec((tm, LANES), lambda i: (i, 0))] * 2,
        out_shape=[jax.ShapeDtypeStruct((t, LANES), F32)] * 2,
        compiler_params=_cparams("parallel"),
        name="rope_tables",
    )(positions.reshape(t, 1).astype(F32), freq, sign)


def _rotate_half_pairs(x):
    n = x.shape[-1]
    lane = lax.broadcasted_iota(jnp.int32, x.shape, x.ndim - 1)
    half = QK_ROPE // 2
    return jnp.where(lane % QK_ROPE < half, pltpu.roll(x, n - half, x.ndim - 1),
                     pltpu.roll(x, half, x.ndim - 1))


def _kv_a_kernel(a_ref, w_ref, g_ref, cos_ref, sin_ref, ckv_ref, krope_ref):
    acc = jnp.dot(a_ref[...], w_ref[...], preferred_element_type=F32)
    c = acc[:, :KV_RANK]
    c = c * lax.rsqrt(jnp.mean(c * c, axis=-1, keepdims=True) + EPS)
    ckv_ref[...] = (c * g_ref[...]).astype(ckv_ref.dtype)
    r = acc[:, KV_RANK:]
    r = r * cos_ref[...] + _rotate_half_pairs(r) * sin_ref[...]
    krope_ref[...] = r[:, :QK_ROPE].astype(krope_ref.dtype)


def kv_a(a, w_kv_a, g_kv, cos_t, sin_t, tm=512):
    m, kd = a.shape
    n = KV_RANK + LANES
    w = jnp.pad(w_kv_a, ((0, 0), (0, n - w_kv_a.shape[1]))).astype(BF16)
    tm = _tile(m, tm)
    return pl.pallas_call(
        _kv_a_kernel,
        grid=(m // tm,),
        in_specs=[pl.BlockSpec((tm, kd), lambda i: (i, 0)),
                  pl.BlockSpec((kd, n), lambda i: (0, 0)),
                  pl.BlockSpec((1, KV_RANK), lambda i: (0, 0)),
                  pl.BlockSpec((tm, LANES), lambda i: (i, 0)),
                  pl.BlockSpec((tm, LANES), lambda i: (i, 0))],
        out_specs=[pl.BlockSpec((tm, KV_RANK), lambda i: (i, 0)),
                   pl.BlockSpec((tm, QK_ROPE), lambda i: (i, 0))],
        out_shape=[jax.ShapeDtypeStruct((m, KV_RANK), BF16),
                   jax.ShapeDtypeStruct((m, QK_ROPE), BF16)],
        compiler_params=_cparams("parallel"),
        name="kv_a",
    )(a, w, g_kv.reshape(1, KV_RANK).astype(F32), cos_t, sin_t)


def _kv_b_kernel(c_ref, kr_ref, w_ref, k_ref, v_ref, *, hb):
    acc = jnp.dot(c_ref[...], w_ref[...], preferred_element_type=F32)
    kr = kr_ref[...]
    hw = QK_NOPE + V_DIM
    for h in range(hb):
        k_ref[h, :, :QK_NOPE] = acc[:, h * hw:h * hw + QK_NOPE].astype(k_ref.dtype)
        k_ref[h, :, QK_NOPE:] = kr
        v_ref[h] = acc[:, h * hw + QK_NOPE:(h + 1) * hw].T.astype(v_ref.dtype)


def kv_b(ckv, krope, w_kv_b, hb=8, tm=1024):
    m = ckv.shape[0]
    hw = QK_NOPE + V_DIM
    heads = w_kv_b.shape[1] // hw
    hb = min(hb, heads)
    tm = _tile(m, tm)
    return pl.pallas_call(
        functools.partial(_kv_b_kernel, hb=hb),
        grid=(m // tm, heads // hb),
        in_specs=[pl.BlockSpec((tm, KV_RANK), lambda i, j: (i, 0)),
                  pl.BlockSpec((tm, QK_ROPE), lambda i, j: (i, 0)),
                  pl.BlockSpec((KV_RANK, hb * hw), lambda i, j: (0, j))],
        out_specs=[pl.BlockSpec((hb, tm, QK_NOPE + QK_ROPE), lambda i, j: (j, i, 0)),
                   pl.BlockSpec((hb, V_DIM, tm), lambda i, j: (j, 0, i))],
        out_shape=[jax.ShapeDtypeStruct((heads, m, QK_NOPE + QK_ROPE), BF16),
                   jax.ShapeDtypeStruct((heads, V_DIM, m), BF16)],
        compiler_params=_cparams("parallel", "parallel"),
        name="kv_b",
    )(ckv, krope, w_kv_b)


def _q_a_kernel(a_ref, w_ref, g_ref, o_ref):
    c = jnp.dot(a_ref[...], w_ref[...], preferred_element_type=F32)
    c = c * lax.rsqrt(jnp.mean(c * c, axis=-1, keepdims=True) + EPS)
    o_ref[...] = (c * g_ref[...]).astype(o_ref.dtype)


def q_a(a, w, layer, g, tm=512):
    m, kd = a.shape
    n = w.shape[2]
    tm = _tile(m, tm)
    return pl.pallas_call(
        _q_a_kernel,
        grid=(m // tm,),
        in_specs=[pl.BlockSpec((tm, kd), lambda i: (i, 0)),
                  pl.BlockSpec((None, kd, n), lambda i: (layer, 0, 0)),
                  pl.BlockSpec((1, n), lambda i: (0, 0))],
        out_specs=pl.BlockSpec((tm, n), lambda i: (i, 0)),
        out_shape=jax.ShapeDtypeStruct((m, n), BF16),
        compiler_params=_cparams("parallel"),
        name="q_a",
    )(a, w, g.reshape(1, n).astype(F32))


def _q_b_kernel(c_ref, w_ref, cos_ref, sin_ref, o_ref, *, hb, scale):
    acc = jnp.dot(c_ref[...], w_ref[...], preferred_element_type=F32)
    nope = acc[:, :hb * QK_NOPE] * scale
    pe = acc[:, hb * QK_NOPE:]
    reps = hb * QK_ROPE // LANES
    cos = jnp.concatenate([cos_ref[...]] * reps, axis=1)
    sin = jnp.concatenate([sin_ref[...]] * reps, axis=1)
    pe = (pe * cos + _rotate_half_pairs(pe) * sin) * scale
    for h in range(hb):
        o_ref[h, :, :QK_NOPE] = nope[:, h * QK_NOPE:(h + 1) * QK_NOPE].astype(o_ref.dtype)
        o_ref[h, :, QK_NOPE:] = pe[:, h * QK_ROPE:(h + 1) * QK_ROPE].astype(o_ref.dtype)


def q_b(cq, w_q_b, cos_t, sin_t, hb=8, tm=1024):
    m, kd = cq.shape
    hd = QK_NOPE + QK_ROPE
    heads = w_q_b.shape[1] // hd
    hb = min(hb, heads)
    assert (hb * QK_ROPE) % LANES == 0
    w3 = w_q_b.reshape(kd, heads // hb, hb, hd)
    w = jnp.concatenate([w3[..., :QK_NOPE].reshape(kd, heads // hb, hb * QK_NOPE),
                         w3[..., QK_NOPE:].reshape(kd, heads // hb, hb * QK_ROPE)], axis=-1)
    gw = hb * hd
    w = w.reshape(kd, (heads // hb) * gw).astype(BF16)
    tm = _tile(m, tm)
    return pl.pallas_call(
        functools.partial(_q_b_kernel, hb=hb, scale=float(hd) ** -0.5 * math.log2(math.e)),
        grid=(m // tm, heads // hb),
        in_specs=[pl.BlockSpec((tm, kd), lambda i, j: (i, 0)),
                  pl.BlockSpec((kd, gw), lambda i, j: (0, j)),
                  pl.BlockSpec((tm, LANES), lambda i, j: (i, 0)),
                  pl.BlockSpec((tm, LANES), lambda i, j: (i, 0))],
        out_specs=pl.BlockSpec((hb, tm, hd), lambda i, j: (j, i, 0)),
        out_shape=jax.ShapeDtypeStruct((heads, m, hd), BF16),
        compiler_params=_cparams("parallel", "parallel"),
        name="q_b",
    )(cq, w, cos_t, sin_t)


def _flash_kernel(q_ref, k_ref, vt_ref, o_ref, s0_ref, s1_ref, m_ref, l_ref, acc_ref, *, tq, nq):
    contract_last = (((1,), (1,)), ((), ()))
    bufs = (s0_ref, s1_ref)
    blocks = [(qi, j) for qi in range(nq) for j in range(qi + 1)]

    def scores(qi, j, s_ref):
        q = q_ref[0, qi * tq:(qi + 1) * tq, :]
        k = k_ref[0, j * tq:(j + 1) * tq, :]
        s_ref[...] = lax.dot_general(k, q, contract_last, preferred_element_type=F32)

    def update(qi, j, s_ref):
        vt = vt_ref[0, :, j * tq:(j + 1) * tq]
        s = s_ref[...]
        if j == qi:
            key = lax.broadcasted_iota(jnp.int32, s.shape, 0)
            qry = lax.broadcasted_iota(jnp.int32, s.shape, 1)
            s = jnp.where(key <= qry, s, -jnp.inf)
        m_cur = jnp.max(s, axis=0, keepdims=True)
        if j == 0:
            p = jnp.exp2(s - m_cur)
            m_ref[...] = m_cur
            l_ref[...] = jnp.sum(p, axis=0, keepdims=True)
            acc_ref[...] = jnp.dot(vt, p.astype(vt.dtype), preferred_element_type=F32)
        else:
            m_prev = m_ref[...]
            m_new = jnp.maximum(m_prev, m_cur)
            p = jnp.exp2(s - m_new)
            alpha = jnp.exp2(m_prev - m_new)
            l_ref[...] = alpha * l_ref[...] + jnp.sum(p, axis=0, keepdims=True)
            acc_ref[...] = alpha * acc_ref[...] + jnp.dot(vt, p.astype(vt.dtype), preferred_element_type=F32)
            m_ref[...] = m_new
        if j == qi:
            o_ref[qi * tq:(qi + 1) * tq, :] = (acc_ref[...] / l_ref[...]).T.astype(o_ref.dtype)

    scores(*blocks[0], bufs[0])
    for n, (qi, j) in enumerate(blocks):
        if n + 1 < len(blocks):
            scores(*blocks[n + 1], bufs[(n + 1) % 2])
        update(qi, j, bufs[n % 2])


def flash_attention(q, k, vt, bsz, seq_len, tq=512):
    heads, t, hd = q.shape
    tq = _tile(seq_len, tq)
    nq = seq_len // tq
    kern = functools.partial(_flash_kernel, tq=tq, nq=nq)
    return pl.pallas_call(
        kern,
        grid=(bsz, heads),
        in_specs=[pl.BlockSpec((1, seq_len, hd), lambda b, h: (h, b, 0)),
                  pl.BlockSpec((1, seq_len, hd), lambda b, h: (h, b, 0)),
                  pl.BlockSpec((1, V_DIM, seq_len), lambda b, h: (h, 0, b))],
        out_specs=pl.BlockSpec((seq_len, V_DIM), lambda b, h: (b, h)),
        out_shape=jax.ShapeDtypeStruct((t, heads * V_DIM), BF16),
        scratch_shapes=[pltpu.VMEM((tq, tq), F32), pltpu.VMEM((tq, tq), F32),
                        pltpu.VMEM((1, tq), F32), pltpu.VMEM((1, tq), F32),
                        pltpu.VMEM((V_DIM, tq), F32)],
        compiler_params=_cparams("parallel", "parallel"),
        name="flash_attention",
    )(q, k, vt)


def kernel(x, positions, ln_mix, ln_ffn, ln_final, ssm_A_re, ssm_A_im, ssm_log_dt, ssm_B_re, ssm_B_im,
           ssm_C_re, ssm_C_im, ssm_D, ssm_w_glu, ssm_b_glu, kv_in_norm, w_kv_a, kv_latent_norm, w_kv_b,
           w_q_a, q_latent_norm, w_q_b, w_o, ffn_w_in, ffn_conv_w, ffn_conv_b, ffn_w_out):
    bsz, seq_len, d = x.shape
    t = bsz * seq_len
    depth = ln_mix.shape[0]
    n_a = ssm_A_re.shape[0]

    xs = x.reshape(t, d).astype(F32)
    cos_t, sin_t = rope_tables(positions)
    k_heads = v_heads = None
    w_kv_b_bf = w_kv_b.astype(BF16)
    w_glu_bf, w_q_a_bf, w_o_bf, w_out_bf = (w.astype(BF16) for w in (ssm_w_glu, w_q_a, w_o, ffn_w_out))
    for l in range(depth):
        h = rmsnorm(xs, ln_mix[l], BF16)
        if l < n_a:
            y = s5_mixer_pre_glu(h, bsz, seq_len, ssm_A_re[l], ssm_A_im[l], ssm_log_dt[l], ssm_B_re[l],
                                 ssm_B_im[l], ssm_C_re[l], ssm_C_im[l], ssm_D[l])
            xs = mm_glu(y, w_glu_bf, l, ssm_b_glu[l], xs)
        else:
            b = l - n_a
            cq = q_a(h, w_q_a_bf, b, q_latent_norm[b])
            q_heads = q_b(cq, w_q_b[b], cos_t, sin_t)
            o = flash_attention(q_heads, k_heads, v_heads, bsz, seq_len)
            xs = mm_residual(o, w_o_bf, b, xs)
        act = ffn_in(rmsnorm(xs, ln_ffn[l], BF16), ffn_w_in, l, ffn_conv_w[l], ffn_conv_b[l], seq_len)
        xs = mm_residual(act, w_out_bf, l, xs)
        if l == n_a - 1:
            ckv, krope = kv_a(rmsnorm(xs, kv_in_norm, BF16), w_kv_a, kv_latent_norm, cos_t, sin_t)
            k_heads, v_heads = kv_b(ckv, krope, w_kv_b_bf)
    return rmsnorm(xs, ln_final, x.dtype).reshape(bsz, seq_len, d)
```

```python
import functools
import math

import jax
import jax.numpy as jnp
from jax import lax
from jax.experimental import pallas as pl
from jax.experimental.pallas import tpu as pltpu

F32 = jnp.float32
BF16 = jnp.bfloat16

EPS = 1e-6
SSM_GROUP = 16
SSM_STATE = 64
MLA_HEADS = 64
KV_RANK = 512
QK_NOPE = 128
QK_ROPE = 64
V_DIM = 128
V_PAD = 16
ROPE_THETA = 10000.0
CONV_W = 3

VMEM_LIMIT_BYTES = 60 * 1024 * 1024
LANES = 128
SSM_CHUNK = 64
STEPS_PER_LANE_TILE = LANES // SSM_GROUP


def _cparams(*sem):
    return pltpu.CompilerParams(dimension_semantics=sem, vmem_limit_bytes=VMEM_LIMIT_BYTES)


def _tile(dim, pref):
    t = min(dim, pref)
    while dim % t:
        t //= 2
    return t


def _rmsnorm_kernel(x_ref, g_ref, o_ref):
    x = x_ref[...].astype(F32)
    y = x * lax.rsqrt(jnp.mean(x * x, axis=-1, keepdims=True) + EPS)
    o_ref[...] = (y * g_ref[...]).astype(o_ref.dtype)


def _rmsnorm_t_kernel(x_ref, g_ref, o_ref):
    x = x_ref[...].astype(F32)
    y = x * lax.rsqrt(jnp.mean(x * x, axis=-1, keepdims=True) + EPS)
    o_ref[...] = (y * g_ref[...]).T.astype(o_ref.dtype)


def rmsnorm(x, g, out_dtype, transposed=False):
    m, d = x.shape
    tm = _tile(m, 512)
    out_spec = pl.BlockSpec((d, tm), lambda i: (0, i)) if transposed else pl.BlockSpec((tm, d), lambda i: (i, 0))
    return pl.pallas_call(
        _rmsnorm_t_kernel if transposed else _rmsnorm_kernel,
        grid=(m // tm,),
        in_specs=[pl.BlockSpec((tm, d), lambda i: (i, 0)),
                  pl.BlockSpec((1, d), lambda i: (0, 0))],
        out_specs=out_spec,
        out_shape=jax.ShapeDtypeStruct((d, m) if transposed else (m, d), out_dtype),
        compiler_params=_cparams("parallel"),
        name="rmsnorm_t" if transposed else "rmsnorm",
    )(x, g.reshape(1, d).astype(F32))


def _mm_res_kernel(a_ref, w_ref, r_ref, o_ref):
    o_ref[...] = r_ref[...] + jnp.dot(a_ref[...], w_ref[...], preferred_element_type=F32)


def mm_residual(a, w, layer, res, tm=512, tn=512):
    m, kd = a.shape
    n = w.shape[2]
    tm, tn = _tile(m, tm), _tile(n, tn)
    return pl.pallas_call(
        _mm_res_kernel,
        grid=(m // tm, n // tn),
        in_specs=[pl.BlockSpec((tm, kd), lambda i, j: (i, 0)),
                  pl.BlockSpec((None, kd, tn), lambda i, j: (layer, 0, j)),
                  pl.BlockSpec((tm, tn), lambda i, j: (i, j))],
        out_specs=pl.BlockSpec((tm, tn), lambda i, j: (i, j)),
        out_shape=jax.ShapeDtypeStruct((m, n), F32),
        compiler_params=_cparams("parallel", "parallel"),
        name="mm_residual",
    )(a, w, res)


def _mm_glu_kernel(a_ref, w1_ref, w2_ref, b1_ref, b2_ref, r_ref, o_ref, *, nsub):
    rs = a_ref.shape[0] // nsub
    for r in range(nsub):
        rows = slice(r * rs, (r + 1) * rs)
        a = a_ref[rows, :]
        z1 = jnp.dot(a, w1_ref[...], preferred_element_type=F32) + b1_ref[...]
        z2 = jnp.dot(a, w2_ref[...], preferred_element_type=F32) + b2_ref[...]
        o_ref[rows, :] = r_ref[rows, :] + z1 * jax.nn.sigmoid(z2)


def mm_glu(a, w, layer, b, res, tm=512, tn=512):
    m, kd = a.shape
    n = w.shape[2] // 2
    tm, tn = _tile(m, tm), _tile(n, tn)
    nj = n // tn
    b2d = b.reshape(1, 2 * n).astype(F32)
    return pl.pallas_call(
        functools.partial(_mm_glu_kernel, nsub=2 if tm % 32 == 0 else 1),
        grid=(m // tm, nj),
        in_specs=[pl.BlockSpec((tm, kd), lambda i, j: (i, 0)),
                  pl.BlockSpec((None, kd, tn), lambda i, j: (layer, 0, j)),
                  pl.BlockSpec((None, kd, tn), lambda i, j: (layer, 0, j + nj)),
                  pl.BlockSpec((1, tn), lambda i, j: (0, j)),
                  pl.BlockSpec((1, tn), lambda i, j: (0, j + nj)),
                  pl.BlockSpec((tm, tn), lambda i, j: (i, j))],
        out_specs=pl.BlockSpec((tm, tn), lambda i, j: (i, j)),
        out_shape=jax.ShapeDtypeStruct((m, n), F32),
        compiler_params=_cparams("parallel", "parallel"),
        name="mm_glu",
    )(a, w, w, b2d, b2d, res)


def _ffn_in_kernel(a_ref, wg_ref, wu_ref, cw_ref, cb_ref, o_ref, wgb_ref, wub_ref, carry_ref, *,
                   tiles_per_seq, nsub):
    i = pl.program_id(1)

    @pl.when(i == 0)
    def _():
        wgb_ref[...] = wg_ref[...].astype(wgb_ref.dtype)
        wub_ref[...] = wu_ref[...].astype(wub_ref.dtype)

    @pl.when(i % tiles_per_seq == 0)
    def _():
        carry_ref[...] = jnp.zeros_like(carry_ref)

    cw = cw_ref[...]
    cb = cb_ref[...]
    rs = a_ref.shape[0] // nsub
    prev = carry_ref[...]
    for r in range(nsub):
        a = a_ref[r * rs:(r + 1) * rs, :]
        gate = jnp.dot(a, wgb_ref[...], preferred_element_type=F32)
        up = jnp.dot(a, wub_ref[...], preferred_element_type=F32)
        row = lax.broadcasted_iota(jnp.int32, gate.shape, 0)
        g1 = jnp.where(row == 0, prev[7:8, :], pltpu.roll(gate, 1, 0))
        g2 = jnp.where(row == 0, prev[6:7, :],
                       jnp.where(row == 1, prev[7:8, :], pltpu.roll(gate, 2, 0)))
        prev = gate[rs - 8:, :]
        conv = cw[0:1, :] * g2 + cw[1:2, :] * g1 + cw[2:3, :] * gate + cb
        o_ref[r * rs:(r + 1) * rs, :] = (jax.nn.silu(conv) * up).astype(o_ref.dtype)
    carry_ref[...] = prev


def ffn_in(a, w_in, layer, conv_w, conv_b, seq_len, tm=2048, tn=256, nsub=4):
    m, kd = a.shape
    f = w_in.shape[2] // 2
    tm, tn = _tile(min(m, seq_len), tm), _tile(f, tn)
    nsub = nsub if tm % nsub == 0 and tm // nsub >= 8 else 1
    assert seq_len % tm == 0 and tm >= 8
    nj = f // tn
    kern = functools.partial(_ffn_in_kernel, tiles_per_seq=seq_len // tm, nsub=nsub)
    return pl.pallas_call(
        kern,
        grid=(nj, m // tm),
        in_specs=[pl.BlockSpec((tm, kd), lambda j, i: (i, 0)),
                  pl.BlockSpec((None, kd, tn), lambda j, i: (layer, 0, j)),
                  pl.BlockSpec((None, kd, tn), lambda j, i: (layer, 0, j + nj)),
                  pl.BlockSpec((CONV_W, tn), lambda j, i: (0, j)),
                  pl.BlockSpec((1, tn), lambda j, i: (0, j))],
        out_specs=pl.BlockSpec((tm, tn), lambda j, i: (i, j)),
        out_shape=jax.ShapeDtypeStruct((m, f), BF16),
        scratch_shapes=[pltpu.VMEM((kd, tn), BF16), pltpu.VMEM((kd, tn), BF16), pltpu.VMEM((8, tn), F32)],
        compiler_params=_cparams("parallel", "arbitrary"),
        name="ffn_in",
    )(a, w_in, w_in, conv_w.astype(F32), conv_b.reshape(1, f).astype(F32))


def _cmul(ar, ai, br, bi):
    return ar * br - ai * bi, ar * bi + ai * br


def _s5_tables_kernel(acol_ref, arow_ref, ldt_ref, bt_re_ref, bt_im_ref, btr_re_ref, btr_im_ref,
                      ct_re_ref, ct_im_ref,
                      pt_re_ref, pt_im_ref, q_re_ref, q_im_ref, tblk_ref, laml_ref, *, chunk):
    nblk = chunk // STEPS_PER_LANE_TILE
    dt = jnp.exp(ldt_ref[0])
    ar, ai = acol_ref[0][:, 0:1], acol_ref[0][:, 1:2]
    adr, adi = ar * dt, ai * dt
    p = ar.shape[0]

    def cpow(e, xr, xi):
        mag = jnp.exp(e * xr)
        return mag * jnp.cos(e * xi), mag * jnp.sin(e * xi)

    lane = lax.broadcasted_iota(jnp.int32, (p, LANES), 1)
    kb = (lane // SSM_GROUP).astype(F32)
    eb_re, eb_im = cpow(kb, adr, adi)
    er_re, er_im = cpow((STEPS_PER_LANE_TILE - 1) - kb, adr, adi)
    lk_re, lk_im = cpow(lane.astype(F32), adr, adi)

    def zoh(lr, li, xr, xi):
        den = xr * xr + xi * xi
        return ((lr - 1.0) * xr + li * xi) / den, (li * xr - (lr - 1.0) * xi) / den

    f_re, f_im = zoh(lk_re[:, 1:2], lk_im[:, 1:2], ar, ai)
    bb_re, bb_im = _cmul(f_re, f_im, bt_re_ref[0], bt_im_ref[0])

    arr, air = arow_ref[0][0:1, :], arow_ref[0][1:2, :]
    one = jnp.ones_like(arr)
    lr_row, li_row = cpow(one, arr * dt, air * dt)
    fr_re, fr_im = zoh(lr_row, li_row, arr, air)
    bbt_re, bbt_im = _cmul(fr_re, fr_im, btr_re_ref[0], btr_im_ref[0])
    ll_re, ll_im = cpow(one * float(chunk), arr * dt, air * dt)
    laml_ref[0] = jnp.concatenate([ll_re, ll_im], axis=0)

    ct_re, ct_im = ct_re_ref[0], ct_im_ref[0]
    ktabs = [jnp.zeros((SSM_GROUP, LANES), F32)]
    for j in range(nblk):
        k0 = j * STEPS_PER_LANE_TILE
        kr = chunk - STEPS_PER_LANE_TILE - k0
        e0 = _cmul(eb_re, eb_im, lk_re[:, k0:k0 + 1], lk_im[:, k0:k0 + 1])
        e1 = _cmul(eb_re, eb_im, lk_re[:, k0 + 1:k0 + 2], lk_im[:, k0 + 1:k0 + 2])
        ev = _cmul(er_re, er_im, lk_re[:, kr:kr + 1], lk_im[:, kr:kr + 1])
        sl = slice(j * LANES, (j + 1) * LANES)
        qr, qi = _cmul(e1[0], e1[1], ct_re, ct_im)
        q_re_ref[0, :, sl] = qr.astype(q_re_ref.dtype)
        q_im_ref[0, :, sl] = (-qi).astype(q_im_ref.dtype)
        pr, pi = _cmul(ev[0], ev[1], bb_re, bb_im)
        pt_re_ref[0, :, sl] = pr.astype(pt_re_ref.dtype)
        pt_im_ref[0, :, sl] = pi.astype(pt_im_ref.dtype)
        wr, wi = _cmul(e0[0], e0[1], ct_re, ct_im)
        ktabs.append(jnp.dot(bbt_re, wr, preferred_element_type=F32, precision=lax.Precision.HIGHEST)
                     - jnp.dot(bbt_im, wi, preferred_element_type=F32, precision=lax.Precision.HIGHEST))
    ktab = jnp.concatenate(ktabs, axis=1)
    for s in range(STEPS_PER_LANE_TILE):
        shifted = pltpu.roll(ktab, SSM_GROUP * s, 1) if s else ktab
        for d in range(nblk):
            tblk_ref[0, d, s * SSM_GROUP:(s + 1) * SSM_GROUP, :] = (
                shifted[:, (d + 1) * LANES:(d + 2) * LANES].astype(tblk_ref.dtype))


def s5_tables(a_re, a_im, log_dt, b_re, b_im, c_re, c_im, chunk):
    g, p = a_re.shape
    c = b_re.shape[-1]
    assert c == SSM_GROUP and chunk % STEPS_PER_LANE_TILE == 0 and chunk < LANES
    w = chunk * c
    nblk = chunk // STEPS_PER_LANE_TILE
    rep = LANES // c
    acol = jnp.stack([a_re, a_im], axis=-1).astype(F32)
    arow = jnp.stack([a_re, a_im], axis=1).astype(F32)
    ldt = log_dt.reshape(g, 1, 1).astype(F32)
    bt_re = jnp.tile(b_re.astype(F32), (1, 1, rep))
    bt_im = jnp.tile(b_im.astype(F32), (1, 1, rep))
    btr_re = jnp.swapaxes(b_re, 1, 2).astype(F32)
    btr_im = jnp.swapaxes(b_im, 1, 2).astype(F32)
    ct_re = jnp.tile(jnp.swapaxes(c_re, 1, 2).astype(F32), (1, 1, rep))
    ct_im = jnp.tile(jnp.swapaxes(c_im, 1, 2).astype(F32), (1, 1, rep))
    g3 = lambda s1, s2: pl.BlockSpec((1, s1, s2), lambda i: (i, 0, 0))
    return pl.pallas_call(
        functools.partial(_s5_tables_kernel, chunk=chunk),
        grid=(g,),
        in_specs=[g3(p, 2), g3(2, p), g3(1, 1), g3(p, LANES), g3(p, LANES), g3(c, p), g3(c, p),
                  g3(p, LANES), g3(p, LANES)],
        out_specs=[g3(p, w), g3(p, w), g3(p, w), g3(p, w),
                   pl.BlockSpec((1, nblk, LANES, LANES), lambda i: (i, 0, 0, 0)),
                   g3(2, p)],
        out_shape=[jax.ShapeDtypeStruct((g, p, w), BF16)] * 4
        + [jax.ShapeDtypeStruct((g, nblk, LANES, LANES), BF16),
           jax.ShapeDtypeStruct((g, 2, p), F32)],
        compiler_params=_cparams("parallel"),
        name="s5_tables",
    )(acol, arow, ldt, bt_re, bt_im, btr_re, btr_im, ct_re, ct_im)


def _gelu_tanh(x):
    c = math.sqrt(2.0 / math.pi)
    return x * (0.5 * (1.0 + jnp.tanh(c * (x + 0.044715 * (x * x * x)))))


def _s5_scan_kernel(u_ref, pt_re_ref, pt_im_ref, q_re_ref, q_im_ref, tblk_ref, laml_ref, dsk_ref, o_ref,
                    m_ref, s_re_ref, s_im_ref, h_re_ref, h_im_ref, *, nblk, nchunks, bsz):
    @pl.when(pl.program_id(0) == 0)
    def _():
        m_ref[...] = jnp.zeros_like(m_ref)

    for i in range(nblk):
        for j in range(i, nblk):
            m_ref[i * LANES:(i + 1) * LANES, j * LANES:(j + 1) * LANES] = tblk_ref[0, j - i]

    u = u_ref[0]
    contract_last = (((1,), (1,)), ((), ()))
    lr, li = laml_ref[0][0:1, :], laml_ref[0][1:2, :]
    p = lr.shape[1]
    pt = jnp.concatenate([pt_re_ref[0], pt_im_ref[0]], axis=0)
    s = lax.dot_general(u, pt, contract_last, preferred_element_type=F32)
    s_re_ref[...] = s[:, :p]
    s_im_ref[...] = s[:, p:]

    hr = jnp.zeros((bsz, p), F32)
    hi = jnp.zeros((bsz, p), F32)
    for c in range(nchunks):
        rows = slice(c * bsz, (c + 1) * bsz)
        h_re_ref[rows, :] = hr
        h_im_ref[rows, :] = hi
        hr, hi = (lr * hr - li * hi + s_re_ref[rows, :],
                  lr * hi + li * hr + s_im_ref[rows, :])

    y = jnp.dot(u, m_ref[...], preferred_element_type=F32)
    h = jnp.concatenate([h_re_ref[...], h_im_ref[...]], axis=1).astype(BF16)
    qt = jnp.concatenate([q_re_ref[0], q_im_ref[0]], axis=0)
    y += jnp.dot(h, qt, preferred_element_type=F32)
    y += dsk_ref[0] * u.astype(F32)
    o_ref[0] = _gelu_tanh(y).astype(o_ref.dtype)


def s5_scan(u_g, tables, d_skip_t, bsz):
    pt_re, pt_im, q_re, q_im, tblk, laml = tables
    g, r, w = u_g.shape
    p = pt_re.shape[1]
    nblk = tblk.shape[1]
    g3 = lambda s1, s2: pl.BlockSpec((1, s1, s2), lambda i: (i, 0, 0))
    kern = functools.partial(_s5_scan_kernel, nblk=nblk, nchunks=r // bsz, bsz=bsz)
    return pl.pallas_call(
        kern,
        grid=(g,),
        in_specs=[g3(r, w), g3(p, w), g3(p, w), g3(p, w), g3(p, w),
                  pl.BlockSpec((1, nblk, LANES, LANES), lambda i: (i, 0, 0, 0)),
                  g3(2, p), g3(1, w)],
        out_specs=g3(r, w),
        out_shape=jax.ShapeDtypeStruct((g, r, w), BF16),
        scratch_shapes=[pltpu.VMEM((w, w), BF16)] + [pltpu.VMEM((r, p), F32)] * 4,
        compiler_params=_cparams("arbitrary"),
        name="s5_scan",
    )(u_g, pt_re, pt_im, q_re, q_im, tblk, laml, d_skip_t)


def s5_mixer_pre_glu(h_t, bsz, seq_len, a_re, a_im, log_dt, b_re, b_im, c_re, c_im, d_skip):
    d, t = h_t.shape
    g = d // SSM_GROUP
    chunk = min(SSM_CHUNK, seq_len)
    nc = seq_len // chunk
    tables = s5_tables(a_re, a_im, log_dt, b_re, b_im, c_re, c_im, chunk)
    u_g = (h_t.reshape(g, SSM_GROUP, bsz, nc, chunk).transpose(0, 3, 2, 4, 1)
           .reshape(g, nc * bsz, chunk * SSM_GROUP))
    d_t = jnp.tile(d_skip.reshape(g, 1, SSM_GROUP).astype(F32), (1, 1, chunk))
    y_g = s5_scan(u_g, tables, d_t, bsz)
    return (y_g.reshape(g, nc, bsz, chunk, SSM_GROUP).transpose(2, 1, 3, 0, 4).reshape(t, d))


def _rope_tables_kernel(pos_ref, freq_ref, sign_ref, cos_ref, sin_ref):
    ang = pos_ref[...] * freq_ref[...]
    cos_ref[...] = jnp.cos(ang)
    sin_ref[...] = jnp.sin(ang) * sign_ref[...]


def rope_tables(positions):
    t = positions.size
    half = QK_ROPE // 2
    inv_freq = ROPE_THETA ** (-jnp.arange(half, dtype=F32) / half)
    freq = jnp.tile(inv_freq, LANES // half).reshape(1, LANES)
    sign = jnp.tile(jnp.concatenate([-jnp.ones(half, F32), jnp.ones(half, F32)]),
                    LANES // QK_ROPE).reshape(1, LANES)
    tm = _tile(t, 2048)
    row = pl.BlockSpec((1, LANES), lambda i: (0, 0))
    return pl.pallas_call(
        _rope_tables_kernel,
        grid=(t // tm,),
        in_specs=[pl.BlockSpec((tm, 1), lambda i: (i, 0)), row, row],
        out_specs=[pl.BlockSpec((tm, LANES), lambda i: (i, 0))] * 2,
        out_shape=[jax.ShapeDtypeStruct((t, LANES), F32)] * 2,
        compiler_params=_cparams("parallel"),
        name="rope_tables",
    )(positions.reshape(t, 1).astype(F32), freq, sign)


def _rotate_half_pairs(x):
    n = x.shape[-1]
    lane = lax.broadcasted_iota(jnp.int32, x.shape, x.ndim - 1)
    half = QK_ROPE // 2
    return jnp.where(lane % QK_ROPE < half, pltpu.roll(x, n - half, x.ndim - 1),
                     pltpu.roll(x, half, x.ndim - 1))


def _kv_a_kernel(a_ref, w_ref, g_ref, cos_ref, sin_ref, ckv_ref, krope_ref):
    acc = jnp.dot(a_ref[...], w_ref[...], preferred_element_type=F32)
    c = acc[:, :KV_RANK]
    c = c * lax.rsqrt(jnp.mean(c * c, axis=-1, keepdims=True) + EPS)
    ckv_ref[...] = (c * g_ref[...]).astype(ckv_ref.dtype)
    r = acc[:, KV_RANK:]
    r = r * cos_ref[...] + _rotate_half_pairs(r) * sin_ref[...]
    krope_ref[...] = r[:, :QK_ROPE].astype(krope_ref.dtype)


def kv_a(a, w_kv_a, g_kv, cos_t, sin_t, tm=512):
    m, kd = a.shape
    n = KV_RANK + LANES
    w = jnp.pad(w_kv_a, ((0, 0), (0, n - w_kv_a.shape[1]))).astype(BF16)
    tm = _tile(m, tm)
    return pl.pallas_call(
        _kv_a_kernel,
        grid=(m // tm,),
        in_specs=[pl.BlockSpec((tm, kd), lambda i: (i, 0)),
                  pl.BlockSpec((kd, n), lambda i: (0, 0)),
                  pl.BlockSpec((1, KV_RANK), lambda i: (0, 0)),
                  pl.BlockSpec((tm, LANES), lambda i: (i, 0)),
                  pl.BlockSpec((tm, LANES), lambda i: (i, 0))],
        out_specs=[pl.BlockSpec((tm, KV_RANK), lambda i: (i, 0)),
                   pl.BlockSpec((tm, QK_ROPE), lambda i: (i, 0))],
        out_shape=[jax.ShapeDtypeStruct((m, KV_RANK), BF16),
                   jax.ShapeDtypeStruct((m, QK_ROPE), BF16)],
        compiler_params=_cparams("parallel"),
        name="kv_a",
    )(a, w, g_kv.reshape(1, KV_RANK).astype(F32), cos_t, sin_t)


def _kv_b_kernel(c_ref, kr_ref, w_ref, k_ref, v_ref, *, hb):
    acc = jnp.dot(c_ref[...], w_ref[...], preferred_element_type=F32)
    kr = kr_ref[...]
    hw = QK_NOPE + V_DIM
    tm = acc.shape[0]
    ones_rows = (lax.broadcasted_iota(jnp.int32, (V_PAD, tm), 0) == 0).astype(v_ref.dtype)
    for h in range(hb):
        k_ref[h, :, :QK_NOPE] = acc[:, h * hw:h * hw + QK_NOPE].astype(k_ref.dtype)
        k_ref[h, :, QK_NOPE:] = kr
        v_ref[h, :V_DIM, :] = acc[:, h * hw + QK_NOPE:(h + 1) * hw].T.astype(v_ref.dtype)
        v_ref[h, V_DIM:, :] = ones_rows


def kv_b(ckv, krope, w_kv_b, hb=8, tm=1024):
    m = ckv.shape[0]
    hw = QK_NOPE + V_DIM
    heads = w_kv_b.shape[1] // hw
    hb = min(hb, heads)
    tm = _tile(m, tm)
    return pl.pallas_call(
        functools.partial(_kv_b_kernel, hb=hb),
        grid=(m // tm, heads // hb),
        in_specs=[pl.BlockSpec((tm, KV_RANK), lambda i, j: (i, 0)),
                  pl.BlockSpec((tm, QK_ROPE), lambda i, j: (i, 0)),
                  pl.BlockSpec((KV_RANK, hb * hw), lambda i, j: (0, j))],
        out_specs=[pl.BlockSpec((hb, tm, QK_NOPE + QK_ROPE), lambda i, j: (j, i, 0)),
                   pl.BlockSpec((hb, V_DIM + V_PAD, tm), lambda i, j: (j, 0, i))],
        out_shape=[jax.ShapeDtypeStruct((heads, m, QK_NOPE + QK_ROPE), BF16),
                   jax.ShapeDtypeStruct((heads, V_DIM + V_PAD, m), BF16)],
        compiler_params=_cparams("parallel", "parallel"),
        name="kv_b",
    )(ckv, krope, w_kv_b)


def _q_a_kernel(a_ref, w_ref, g_ref, o_ref):
    c = jnp.dot(a_ref[...], w_ref[...], preferred_element_type=F32)
    c = c * lax.rsqrt(jnp.mean(c * c, axis=-1, keepdims=True) + EPS)
    o_ref[...] = (c * g_ref[...]).astype(o_ref.dtype)


def q_a(a, w, layer, g, tm=512):
    m, kd = a.shape
    n = w.shape[2]
    tm = _tile(m, tm)
    return pl.pallas_call(
        _q_a_kernel,
        grid=(m // tm,),
        in_specs=[pl.BlockSpec((tm, kd), lambda i: (i, 0)),
                  pl.BlockSpec((None, kd, n), lambda i: (layer, 0, 0)),
                  pl.BlockSpec((1, n), lambda i: (0, 0))],
        out_specs=pl.BlockSpec((tm, n), lambda i: (i, 0)),
        out_shape=jax.ShapeDtypeStruct((m, n), BF16),
        compiler_params=_cparams("parallel"),
        name="q_a",
    )(a, w, g.reshape(1, n).astype(F32))


def _q_b_kernel(c_ref, w_ref, cos_ref, sin_ref, o_ref, *, hb, scale, nsub):
    reps = hb * QK_ROPE // LANES
    rs = c_ref.shape[0] // nsub
    for r in range(nsub):
        rows = slice(r * rs, (r + 1) * rs)
        acc = jnp.dot(c_ref[rows, :], w_ref[...], preferred_element_type=F32)
        nope = acc[:, :hb * QK_NOPE] * scale
        pe = acc[:, hb * QK_NOPE:]
        cos = jnp.concatenate([cos_ref[rows, :]] * reps, axis=1)
        sin = jnp.concatenate([sin_ref[rows, :]] * reps, axis=1)
        pe = (pe * cos + _rotate_half_pairs(pe) * sin) * scale
        for h in range(hb):
            o_ref[h, rows, :QK_NOPE] = nope[:, h * QK_NOPE:(h + 1) * QK_NOPE].astype(o_ref.dtype)
            o_ref[h, rows, QK_NOPE:] = pe[:, h * QK_ROPE:(h + 1) * QK_ROPE].astype(o_ref.dtype)


def q_b(cq, w_q_b, cos_t, sin_t, hb=8, tm=1024):
    m, kd = cq.shape
    hd = QK_NOPE + QK_ROPE
    heads = w_q_b.shape[1] // hd
    hb = min(hb, heads)
    assert (hb * QK_ROPE) % LANES == 0
    w3 = w_q_b.reshape(kd, heads // hb, hb, hd)
    w = jnp.concatenate([w3[..., :QK_NOPE].reshape(kd, heads // hb, hb * QK_NOPE),
                         w3[..., QK_NOPE:].reshape(kd, heads // hb, hb * QK_ROPE)], axis=-1)
    gw = hb * hd
    w = w.reshape(kd, (heads // hb) * gw).astype(BF16)
    tm = _tile(m, tm)
    return pl.pallas_call(
        functools.partial(_q_b_kernel, hb=hb, scale=float(hd) ** -0.5 * math.log2(math.e),
                          nsub=4 if tm % 64 == 0 else 1),
        grid=(m // tm, heads // hb),
        in_specs=[pl.BlockSpec((tm, kd), lambda i, j: (i, 0)),
                  pl.BlockSpec((kd, gw), lambda i, j: (0, j)),
                  pl.BlockSpec((tm, LANES), lambda i, j: (i, 0)),
                  pl.BlockSpec((tm, LANES), lambda i, j: (i, 0))],
        out_specs=pl.BlockSpec((hb, tm, hd), lambda i, j: (j, i, 0)),
        out_shape=jax.ShapeDtypeStruct((heads, m, hd), BF16),
        compiler_params=_cparams("parallel", "parallel"),
        name="q_b",
    )(cq, w, cos_t, sin_t)


def _flash_kernel(q_ref, k_ref, vt_ref, o_ref, s0_ref, s1_ref, m_ref, acc_ref, *, tq, tk, nq):
    contract_last = (((1,), (1,)), ((), ()))
    bufs = (s0_ref, s1_ref)
    ratio = tq // tk
    blocks = [(qi, j, max(0, j - ratio * qi) * tk) for qi in range(nq) for j in range(ratio * qi + ratio)]

    def scores(qi, j, q_lo, s_ref):
        q = q_ref[0, qi * tq + q_lo:(qi + 1) * tq, :]
        k = k_ref[0, j * tk:(j + 1) * tk, :]
        s_ref[:, q_lo:] = lax.dot_general(k, q, contract_last, preferred_element_type=F32)

    def update(qi, j, q_lo, s_ref):
        vt = vt_ref[0, :, j * tk:(j + 1) * tk]
        s = s_ref[:, q_lo:]
        if j >= ratio * qi:
            key = j * tk + lax.broadcasted_iota(jnp.int32, s.shape, 0)
            qry = qi * tq + q_lo + lax.broadcasted_iota(jnp.int32, s.shape, 1)
            s = jnp.where(key <= qry, s, -jnp.inf)
        m_cur = jnp.max(s, axis=0, keepdims=True)
        if j == 0:
            p = jnp.exp2(s - m_cur)
            m_ref[...] = m_cur
            acc_ref[...] = jnp.dot(vt, p.astype(vt.dtype), preferred_element_type=F32)
        else:
            m_prev = m_ref[:, q_lo:]
            m_new = jnp.maximum(m_prev, m_cur)
            p = jnp.exp2(s - m_new)
            alpha = jnp.exp2(m_prev - m_new)
            acc_ref[:, q_lo:] = (alpha * acc_ref[:, q_lo:]
                                 + jnp.dot(vt, p.astype(vt.dtype), preferred_element_type=F32))
            m_ref[:, q_lo:] = m_new
        if j == ratio * qi + ratio - 1:
            acc = acc_ref[...]
            o_ref[qi * tq:(qi + 1) * tq, :] = (acc[:V_DIM] / acc[V_DIM:V_DIM + 1]).T.astype(o_ref.dtype)

    scores(*blocks[0], bufs[0])
    for n, blk in enumerate(blocks):
        if n + 1 < len(blocks):
            scores(*blocks[n + 1], bufs[(n + 1) % 2])
        update(*blk, bufs[n % 2])


def flash_attention(q, k, vt, bsz, seq_len, tq=512, tk=512):
    heads, t, hd = q.shape
    vrows = vt.shape[1]
    tq = _tile(seq_len, tq)
    tk = _tile(tq, tk)
    nq = seq_len // tq
    kern = functools.partial(_flash_kernel, tq=tq, tk=tk, nq=nq)
    return pl.pallas_call(
        kern,
        grid=(bsz, heads),
        in_specs=[pl.BlockSpec((1, seq_len, hd), lambda b, h: (h, b, 0)),
                  pl.BlockSpec((1, seq_len, hd), lambda b, h: (h, b, 0)),
                  pl.BlockSpec((1, vrows, seq_len), lambda b, h: (h, 0, b))],
        out_specs=pl.BlockSpec((seq_len, V_DIM), lambda b, h: (b, h)),
        out_shape=jax.ShapeDtypeStruct((t, heads * V_DIM), BF16),
        scratch_shapes=[pltpu.VMEM((tk, tq), F32), pltpu.VMEM((tk, tq), F32),
                        pltpu.VMEM((1, tq), F32), pltpu.VMEM((vrows, tq), F32)],
        compiler_params=_cparams("parallel", "parallel"),
        name="flash_attention",
    )(q, k, vt)


def kernel(x, positions, ln_mix, ln_ffn, ln_final, ssm_A_re, ssm_A_im, ssm_log_dt, ssm_B_re, ssm_B_im,
           ssm_C_re, ssm_C_im, ssm_D, ssm_w_glu, ssm_b_glu, kv_in_norm, w_kv_a, kv_latent_norm, w_kv_b,
           w_q_a, q_latent_norm, w_q_b, w_o, ffn_w_in, ffn_conv_w, ffn_conv_b, ffn_w_out):
    bsz, seq_len, d = x.shape
    t = bsz * seq_len
    depth = ln_mix.shape[0]
    n_a = ssm_A_re.shape[0]

    xs = x.reshape(t, d).astype(F32)
    cos_t, sin_t = rope_tables(positions)
    k_heads = v_heads = None
    w_kv_b_bf = w_kv_b.astype(BF16)
    w_glu_bf, w_q_a_bf, w_o_bf, w_out_bf = (w.astype(BF16) for w in (ssm_w_glu, w_q_a, w_o, ffn_w_out))
    for l in range(depth):
        h = rmsnorm(xs, ln_mix[l], BF16, transposed=l < n_a)
        if l < n_a:
            y = s5_mixer_pre_glu(h, bsz, seq_len, ssm_A_re[l], ssm_A_im[l], ssm_log_dt[l], ssm_B_re[l],
                                 ssm_B_im[l], ssm_C_re[l], ssm_C_im[l], ssm_D[l])
            xs = mm_glu(y, w_glu_bf, l, ssm_b_glu[l], xs)
        else:
            b = l - n_a
            cq = q_a(h, w_q_a_bf, b, q_latent_norm[b])
            q_heads = q_b(cq, w_q_b[b], cos_t, sin_t)
            o = flash_attention(q_heads, k_heads, v_heads, bsz, seq_len)
            xs = mm_residual(o, w_o_bf, b, xs)
        act = ffn_in(rmsnorm(xs, ln_ffn[l], BF16), ffn_w_in, l, ffn_conv_w[l], ffn_conv_b[l], seq_len)
        xs = mm_residual(act, w_out_bf, l, xs)
        if l == n_a - 1:
            ckv, krope = kv_a(rmsnorm(xs, kv_in_norm, BF16), w_kv_a, kv_latent_norm, cos_t, sin_t)
            k_heads, v_heads = kv_b(ckv, krope, w_kv_b_bf)
    return rmsnorm(xs, ln_final, x.dtype).reshape(bsz, seq_len, d)
```

```python
import functools
import math

import jax
import jax.numpy as jnp
from jax import lax
from jax.experimental import pallas as pl
from jax.experimental.pallas import tpu as pltpu

F32 = jnp.float32
BF16 = jnp.bfloat16

EPS = 1e-6
SSM_GROUP = 16
SSM_STATE = 64
MLA_HEADS = 64
KV_RANK = 512
QK_NOPE = 128
QK_ROPE = 64
V_DIM = 128
V_PAD = 16
ROPE_THETA = 10000.0
CONV_W = 3

VMEM_LIMIT_BYTES = 60 * 1024 * 1024
LANES = 128
SSM_CHUNK = 64
STEPS_PER_LANE_TILE = LANES // SSM_GROUP


def _cparams(*sem):
    return pltpu.CompilerParams(dimension_semantics=sem, vmem_limit_bytes=VMEM_LIMIT_BYTES)


def _tile(dim, pref):
    t = min(dim, pref)
    while dim % t:
        t //= 2
    return t


def _rmsnorm_kernel(x_ref, g_ref, o_ref):
    x = x_ref[...].astype(F32)
    y = x * lax.rsqrt(jnp.mean(x * x, axis=-1, keepdims=True) + EPS)
    o_ref[...] = (y * g_ref[...]).astype(o_ref.dtype)


def _rmsnorm_t_kernel(x_ref, g_ref, o_ref):
    x = x_ref[...].astype(F32)
    y = x * lax.rsqrt(jnp.mean(x * x, axis=-1, keepdims=True) + EPS)
    o_ref[...] = (y * g_ref[...]).T.astype(o_ref.dtype)


def rmsnorm(x, g, out_dtype, transposed=False):
    m, d = x.shape
    tm = _tile(m, 512)
    out_spec = pl.BlockSpec((d, tm), lambda i: (0, i)) if transposed else pl.BlockSpec((tm, d), lambda i: (i, 0))
    return pl.pallas_call(
        _rmsnorm_t_kernel if transposed else _rmsnorm_kernel,
        grid=(m // tm,),
        in_specs=[pl.BlockSpec((tm, d), lambda i: (i, 0)),
                  pl.BlockSpec((1, d), lambda i: (0, 0))],
        out_specs=out_spec,
        out_shape=jax.ShapeDtypeStruct((d, m) if transposed else (m, d), out_dtype),
        compiler_params=_cparams("parallel"),
        name="rmsnorm_t" if transposed else "rmsnorm",
    )(x, g.reshape(1, d).astype(F32))


def _mm_res_kernel(a_ref, w_ref, r_ref, o_ref):
    o_ref[...] = r_ref[...] + jnp.dot(a_ref[...], w_ref[...], preferred_element_type=F32)


def mm_residual(a, w, layer, res, tm=512, tn=512):
    m, kd = a.shape
    n = w.shape[2]
    tm, tn = _tile(m, tm), _tile(n, tn)
    return pl.pallas_call(
        _mm_res_kernel,
        grid=(m // tm, n // tn),
        in_specs=[pl.BlockSpec((tm, kd), lambda i, j: (i, 0)),
                  pl.BlockSpec((None, kd, tn), lambda i, j: (layer, 0, j)),
                  pl.BlockSpec((tm, tn), lambda i, j: (i, j))],
        out_specs=pl.BlockSpec((tm, tn), lambda i, j: (i, j)),
        out_shape=jax.ShapeDtypeStruct((m, n), F32),
        compiler_params=_cparams("parallel", "parallel"),
        name="mm_residual",
    )(a, w, res)


def _mm_glu_kernel(a_ref, w1_ref, w2_ref, b1_ref, b2_ref, r_ref, o_ref, *, nsub):
    rs = a_ref.shape[0] // nsub
    for r in range(nsub):
        rows = slice(r * rs, (r + 1) * rs)
        a = a_ref[rows, :]
        z1 = jnp.dot(a, w1_ref[...], preferred_element_type=F32) + b1_ref[...]
        z2 = jnp.dot(a, w2_ref[...], preferred_element_type=F32) + b2_ref[...]
        o_ref[rows, :] = r_ref[rows, :] + z1 * jax.nn.sigmoid(z2)


def mm_glu(a, w, layer, b, res, tm=512, tn=512):
    m, kd = a.shape
    n = w.shape[2] // 2
    tm, tn = _tile(m, tm), _tile(n, tn)
    nj = n // tn
    b2d = b.reshape(1, 2 * n).astype(F32)
    return pl.pallas_call(
        functools.partial(_mm_glu_kernel, nsub=2 if tm % 32 == 0 else 1),
        grid=(m // tm, nj),
        in_specs=[pl.BlockSpec((tm, kd), lambda i, j: (i, 0)),
                  pl.BlockSpec((None, kd, tn), lambda i, j: (layer, 0, j)),
                  pl.BlockSpec((None, kd, tn), lambda i, j: (layer, 0, j + nj)),
                  pl.BlockSpec((1, tn), lambda i, j: (0, j)),
                  pl.BlockSpec((1, tn), lambda i, j: (0, j + nj)),
                  pl.BlockSpec((tm, tn), lambda i, j: (i, j))],
        out_specs=pl.BlockSpec((tm, tn), lambda i, j: (i, j)),
        out_shape=jax.ShapeDtypeStruct((m, n), F32),
        compiler_params=_cparams("parallel", "parallel"),
        name="mm_glu",
    )(a, w, w, b2d, b2d, res)


def _ffn_in_kernel(a_ref, wg_ref, wu_ref, cw_ref, cb_ref, o_ref, wgb_ref, wub_ref, carry_ref, *,
                   tiles_per_seq, nsub):
    i = pl.program_id(1)

    @pl.when(i == 0)
    def _():
        wgb_ref[...] = wg_ref[...].astype(wgb_ref.dtype)
        wub_ref[...] = wu_ref[...].astype(wub_ref.dtype)

    @pl.when(i % tiles_per_seq == 0)
    def _():
        carry_ref[...] = jnp.zeros_like(carry_ref)

    cw = cw_ref[...]
    cb = cb_ref[...]
    rs = a_ref.shape[0] // nsub
    prev = carry_ref[...]
    for r in range(nsub):
        a = a_ref[r * rs:(r + 1) * rs, :]
        gate = jnp.dot(a, wgb_ref[...], preferred_element_type=F32)
        up = jnp.dot(a, wub_ref[...], preferred_element_type=F32)
        row = lax.broadcasted_iota(jnp.int32, gate.shape, 0)
        g1 = jnp.where(row == 0, prev[7:8, :], pltpu.roll(gate, 1, 0))
        g2 = jnp.where(row == 0, prev[6:7, :],
                       jnp.where(row == 1, prev[7:8, :], pltpu.roll(gate, 2, 0)))
        prev = gate[rs - 8:, :]
        conv = cw[0:1, :] * g2 + cw[1:2, :] * g1 + cw[2:3, :] * gate + cb
        o_ref[r * rs:(r + 1) * rs, :] = (jax.nn.silu(conv) * up).astype(o_ref.dtype)
    carry_ref[...] = prev


def ffn_in(a, w_in, layer, conv_w, conv_b, seq_len, tm=2048, tn=256, nsub=4):
    m, kd = a.shape
    f = w_in.shape[2] // 2
    tm, tn = _tile(min(m, seq_len), tm), _tile(f, tn)
    nsub = nsub if tm % nsub == 0 and tm // nsub >= 8 else 1
    assert seq_len % tm == 0 and tm >= 8
    nj = f // tn
    kern = functools.partial(_ffn_in_kernel, tiles_per_seq=seq_len // tm, nsub=nsub)
    return pl.pallas_call(
        kern,
        grid=(nj, m // tm),
        in_specs=[pl.BlockSpec((tm, kd), lambda j, i: (i, 0)),
                  pl.BlockSpec((None, kd, tn), lambda j, i: (layer, 0, j)),
                  pl.BlockSpec((None, kd, tn), lambda j, i: (layer, 0, j + nj)),
                  pl.BlockSpec((CONV_W, tn), lambda j, i: (0, j)),
                  pl.BlockSpec((1, tn), lambda j, i: (0, j))],
        out_specs=pl.BlockSpec((tm, tn), lambda j, i: (i, j)),
        out_shape=jax.ShapeDtypeStruct((m, f), BF16),
        scratch_shapes=[pltpu.VMEM((kd, tn), BF16), pltpu.VMEM((kd, tn), BF16), pltpu.VMEM((8, tn), F32)],
        compiler_params=_cparams("parallel", "arbitrary"),
        name="ffn_in",
    )(a, w_in, w_in, conv_w.astype(F32), conv_b.reshape(1, f).astype(F32))


def _cmul(ar, ai, br, bi):
    return ar * br - ai * bi, ar * bi + ai * br


def _s5_tables_kernel(acol_ref, arow_ref, ldt_ref, bt_re_ref, bt_im_ref, btr_re_ref, btr_im_ref,
                      ct_re_ref, ct_im_ref,
                      pt_re_ref, pt_im_ref, q_re_ref, q_im_ref, tblk_ref, laml_ref, *, chunk):
    nblk = chunk // STEPS_PER_LANE_TILE
    dt = jnp.exp(ldt_ref[0])
    ar, ai = acol_ref[0][:, 0:1], acol_ref[0][:, 1:2]
    adr, adi = ar * dt, ai * dt
    p = ar.shape[0]

    def cpow(e, xr, xi):
        mag = jnp.exp(e * xr)
        return mag * jnp.cos(e * xi), mag * jnp.sin(e * xi)

    lane = lax.broadcasted_iota(jnp.int32, (p, LANES), 1)
    kb = (lane // SSM_GROUP).astype(F32)
    eb_re, eb_im = cpow(kb, adr, adi)
    er_re, er_im = cpow((STEPS_PER_LANE_TILE - 1) - kb, adr, adi)
    lk_re, lk_im = cpow(lane.astype(F32), adr, adi)

    def zoh(lr, li, xr, xi):
        den = xr * xr + xi * xi
        return ((lr - 1.0) * xr + li * xi) / den, (li * xr - (lr - 1.0) * xi) / den

    f_re, f_im = zoh(lk_re[:, 1:2], lk_im[:, 1:2], ar, ai)
    bb_re, bb_im = _cmul(f_re, f_im, bt_re_ref[0], bt_im_ref[0])

    arr, air = arow_ref[0][0:1, :], arow_ref[0][1:2, :]
    one = jnp.ones_like(arr)
    lr_row, li_row = cpow(one, arr * dt, air * dt)
    fr_re, fr_im = zoh(lr_row, li_row, arr, air)
    bbt_re, bbt_im = _cmul(fr_re, fr_im, btr_re_ref[0], btr_im_ref[0])
    ll_re, ll_im = cpow(one * float(chunk), arr * dt, air * dt)
    laml_ref[0] = jnp.concatenate([ll_re, ll_im], axis=0)

    ct_re, ct_im = ct_re_ref[0], ct_im_ref[0]
    ktabs = [jnp.zeros((SSM_GROUP, LANES), F32)]
    for j in range(nblk):
        k0 = j * STEPS_PER_LANE_TILE
        kr = chunk - STEPS_PER_LANE_TILE - k0
        e0 = _cmul(eb_re, eb_im, lk_re[:, k0:k0 + 1], lk_im[:, k0:k0 + 1])
        e1 = _cmul(eb_re, eb_im, lk_re[:, k0 + 1:k0 + 2], lk_im[:, k0 + 1:k0 + 2])
        ev = _cmul(er_re, er_im, lk_re[:, kr:kr + 1], lk_im[:, kr:kr + 1])
        sl = slice(j * LANES, (j + 1) * LANES)
        qr, qi = _cmul(e1[0], e1[1], ct_re, ct_im)
        q_re_ref[0, :, sl] = qr.astype(q_re_ref.dtype)
        q_im_ref[0, :, sl] = (-qi).astype(q_im_ref.dtype)
        pr, pi = _cmul(ev[0], ev[1], bb_re, bb_im)
        pt_re_ref[0, :, sl] = pr.astype(pt_re_ref.dtype)
        pt_im_ref[0, :, sl] = pi.astype(pt_im_ref.dtype)
        wr, wi = _cmul(e0[0], e0[1], ct_re, ct_im)
        ktabs.append(jnp.dot(bbt_re, wr, preferred_element_type=F32, precision=lax.Precision.HIGHEST)
                     - jnp.dot(bbt_im, wi, preferred_element_type=F32, precision=lax.Precision.HIGHEST))
    ktab = jnp.concatenate(ktabs, axis=1)
    for s in range(STEPS_PER_LANE_TILE):
        shifted = pltpu.roll(ktab, SSM_GROUP * s, 1) if s else ktab
        for d in range(nblk):
            tblk_ref[0, d, s * SSM_GROUP:(s + 1) * SSM_GROUP, :] = (
                shifted[:, (d + 1) * LANES:(d + 2) * LANES].astype(tblk_ref.dtype))


def s5_tables(a_re, a_im, log_dt, b_re, b_im, c_re, c_im, chunk):
    g, p = a_re.shape
    c = b_re.shape[-1]
    assert c == SSM_GROUP and chunk % STEPS_PER_LANE_TILE == 0 and chunk < LANES
    w = chunk * c
    nblk = chunk // STEPS_PER_LANE_TILE
    rep = LANES // c
    acol = jnp.stack([a_re, a_im], axis=-1).astype(F32)
    arow = jnp.stack([a_re, a_im], axis=1).astype(F32)
    ldt = log_dt.reshape(g, 1, 1).astype(F32)
    bt_re = jnp.tile(b_re.astype(F32), (1, 1, rep))
    bt_im = jnp.tile(b_im.astype(F32), (1, 1, rep))
    btr_re = jnp.swapaxes(b_re, 1, 2).astype(F32)
    btr_im = jnp.swapaxes(b_im, 1, 2).astype(F32)
    ct_re = jnp.tile(jnp.swapaxes(c_re, 1, 2).astype(F32), (1, 1, rep))
    ct_im = jnp.tile(jnp.swapaxes(c_im, 1, 2).astype(F32), (1, 1, rep))
    g3 = lambda s1, s2: pl.BlockSpec((1, s1, s2), lambda i: (i, 0, 0))
    return pl.pallas_call(
        functools.partial(_s5_tables_kernel, chunk=chunk),
        grid=(g,),
        in_specs=[g3(p, 2), g3(2, p), g3(1, 1), g3(p, LANES), g3(p, LANES), g3(c, p), g3(c, p),
                  g3(p, LANES), g3(p, LANES)],
        out_specs=[g3(p, w), g3(p, w), g3(p, w), g3(p, w),
                   pl.BlockSpec((1, nblk, LANES, LANES), lambda i: (i, 0, 0, 0)),
                   g3(2, p)],
        out_shape=[jax.ShapeDtypeStruct((g, p, w), BF16)] * 4
        + [jax.ShapeDtypeStruct((g, nblk, LANES, LANES), BF16),
           jax.ShapeDtypeStruct((g, 2, p), F32)],
        compiler_params=_cparams("parallel"),
        name="s5_tables",
    )(acol, arow, ldt, bt_re, bt_im, btr_re, btr_im, ct_re, ct_im)


def _gelu_tanh(x):
    c = math.sqrt(2.0 / math.pi)
    return x * (0.5 * (1.0 + jnp.tanh(c * (x + 0.044715 * (x * x * x)))))


def _s5_scan_kernel(u_ref, pt_re_ref, pt_im_ref, q_re_ref, q_im_ref, tblk_ref, laml_ref, dsk_ref, o_ref,
                    m_ref, s_re_ref, s_im_ref, h_re_ref, h_im_ref, *, nblk, nchunks, bsz):
    @pl.when(pl.program_id(0) == 0)
    def _():
        m_ref[...] = jnp.zeros_like(m_ref)

    for i in range(nblk):
        for j in range(i, nblk):
            m_ref[i * LANES:(i + 1) * LANES, j * LANES:(j + 1) * LANES] = tblk_ref[0, j - i]

    u = u_ref[0]
    contract_last = (((1,), (1,)), ((), ()))
    lr, li = laml_ref[0][0:1, :], laml_ref[0][1:2, :]
    p = lr.shape[1]
    pt = jnp.concatenate([pt_re_ref[0], pt_im_ref[0]], axis=0)
    s = lax.dot_general(u, pt, contract_last, preferred_element_type=F32)
    s_re_ref[...] = s[:, :p]
    s_im_ref[...] = s[:, p:]

    hr = jnp.zeros((bsz, p), F32)
    hi = jnp.zeros((bsz, p), F32)
    for c in range(nchunks):
        rows = slice(c * bsz, (c + 1) * bsz)
        h_re_ref[rows, :] = hr
        h_im_ref[rows, :] = hi
        hr, hi = (lr * hr - li * hi + s_re_ref[rows, :],
                  lr * hi + li * hr + s_im_ref[rows, :])

    y = jnp.dot(u, m_ref[...], preferred_element_type=F32)
    h = jnp.concatenate([h_re_ref[...], h_im_ref[...]], axis=1).astype(BF16)
    qt = jnp.concatenate([q_re_ref[0], q_im_ref[0]], axis=0)
    y += jnp.dot(h, qt, preferred_element_type=F32)
    y += dsk_ref[0] * u.astype(F32)
    o_ref[0] = _gelu_tanh(y).astype(o_ref.dtype)


def s5_scan(u_g, tables, d_skip_t, bsz):
    pt_re, pt_im, q_re, q_im, tblk, laml = tables
    g, r, w = u_g.shape
    p = pt_re.shape[1]
    nblk = tblk.shape[1]
    g3 = lambda s1, s2: pl.BlockSpec((1, s1, s2), lambda i: (i, 0, 0))
    kern = functools.partial(_s5_scan_kernel, nblk=nblk, nchunks=r // bsz, bsz=bsz)
    return pl.pallas_call(
        kern,
        grid=(g,),
        in_specs=[g3(r, w), g3(p, w), g3(p, w), g3(p, w), g3(p, w),
                  pl.BlockSpec((1, nblk, LANES, LANES), lambda i: (i, 0, 0, 0)),
                  g3(2, p), g3(1, w)],
        out_specs=g3(r, w),
        out_shape=jax.ShapeDtypeStruct((g, r, w), BF16),
        scratch_shapes=[pltpu.VMEM((w, w), BF16)] + [pltpu.VMEM((r, p), F32)] * 4,
        compiler_params=_cparams("arbitrary"),
        name="s5_scan",
    )(u_g, pt_re, pt_im, q_re, q_im, tblk, laml, d_skip_t)


def s5_mixer_pre_glu(h_t, bsz, seq_len, a_re, a_im, log_dt, b_re, b_im, c_re, c_im, d_skip):
    d, t = h_t.shape
    g = d // SSM_GROUP
    chunk = min(SSM_CHUNK, seq_len)
    nc = seq_len // chunk
    tables = s5_tables(a_re, a_im, log_dt, b_re, b_im, c_re, c_im, chunk)
    u_g = (h_t.reshape(g, SSM_GROUP, bsz, nc, chunk).transpose(0, 3, 2, 4, 1)
           .reshape(g, nc * bsz, chunk * SSM_GROUP))
    d_t = jnp.tile(d_skip.reshape(g, 1, SSM_GROUP).astype(F32), (1, 1, chunk))
    y_g = s5_scan(u_g, tables, d_t, bsz)
    return (y_g.reshape(g, nc, bsz, chunk, SSM_GROUP).transpose(2, 1, 3, 0, 4).reshape(t, d))


def _rope_tables_kernel(pos_ref, freq_ref, sign_ref, cos_ref, sin_ref):
    ang = pos_ref[...] * freq_ref[...]
    cos_ref[...] = jnp.cos(ang)
    sin_ref[...] = jnp.sin(ang) * sign_ref[...]


def rope_tables(positions):
    t = positions.size
    half = QK_ROPE // 2
    inv_freq = ROPE_THETA ** (-jnp.arange(half, dtype=F32) / half)
    freq = jnp.tile(inv_freq, LANES // half).reshape(1, LANES)
    sign = jnp.tile(jnp.concatenate([-jnp.ones(half, F32), jnp.ones(half, F32)]),
                    LANES // QK_ROPE).reshape(1, LANES)
    tm = _tile(t, 2048)
    row = pl.BlockSpec((1, LANES), lambda i: (0, 0))
    return pl.pallas_call(
        _rope_tables_kernel,
        grid=(t // tm,),
        in_specs=[pl.BlockSpec((tm, 1), lambda i: (i, 0)), row, row],
        out_specs=[pl.BlockSpec((tm, LANES), lambda i: (i, 0))] * 2,
        out_shape=[jax.ShapeDtypeStruct((t, LANES), F32)] * 2,
        compiler_params=_cparams("parallel"),
        name="rope_tables",
    )(positions.reshape(t, 1).astype(F32), freq, sign)


def _rotate_half_pairs(x):
    n = x.shape[-1]
    lane = lax.broadcasted_iota(jnp.int32, x.shape, x.ndim - 1)
    half = QK_ROPE // 2
    return jnp.where(lane % QK_ROPE < half, pltpu.roll(x, n - half, x.ndim - 1),
                     pltpu.roll(x, half, x.ndim - 1))


def _kv_a_kernel(a_ref, w_ref, g_ref, cos_ref, sin_ref, ckv_ref, krope_ref):
    acc = jnp.dot(a_ref[...], w_ref[...], preferred_element_type=F32)
    c = acc[:, :KV_RANK]
    c = c * lax.rsqrt(jnp.mean(c * c, axis=-1, keepdims=True) + EPS)
    ckv_ref[...] = (c * g_ref[...]).astype(ckv_ref.dtype)
    r = acc[:, KV_RANK:]
    r = r * cos_ref[...] + _rotate_half_pairs(r) * sin_ref[...]
    krope_ref[...] = r[:, :QK_ROPE].astype(krope_ref.dtype)


def kv_a(a, w_kv_a, g_kv, cos_t, sin_t, tm=512):
    m, kd = a.shape
    n = KV_RANK + LANES
    w = jnp.pad(w_kv_a, ((0, 0), (0, n - w_kv_a.shape[1]))).astype(BF16)
    tm = _tile(m, tm)
    return pl.pallas_call(
        _kv_a_kernel,
        grid=(m // tm,),
        in_specs=[pl.BlockSpec((tm, kd), lambda i: (i, 0)),
                  pl.BlockSpec((kd, n), lambda i: (0, 0)),
                  pl.BlockSpec((1, KV_RANK), lambda i: (0, 0)),
                  pl.BlockSpec((tm, LANES), lambda i: (i, 0)),
                  pl.BlockSpec((tm, LANES), lambda i: (i, 0))],
        out_specs=[pl.BlockSpec((tm, KV_RANK), lambda i: (i, 0)),
                   pl.BlockSpec((tm, QK_ROPE), lambda i: (i, 0))],
        out_shape=[jax.ShapeDtypeStruct((m, KV_RANK), BF16),
                   jax.ShapeDtypeStruct((m, QK_ROPE), BF16)],
        compiler_params=_cparams("parallel"),
        name="kv_a",
    )(a, w, g_kv.reshape(1, KV_RANK).astype(F32), cos_t, sin_t)


def _kv_b_kernel(c_ref, kr_ref, w_ref, k_ref, v_ref, *, hb):
    acc = jnp.dot(c_ref[...], w_ref[...], preferred_element_type=F32)
    kr = kr_ref[...]
    hw = QK_NOPE + V_DIM
    tm = acc.shape[0]
    ones_rows = (lax.broadcasted_iota(jnp.int32, (V_PAD, tm), 0) == 0).astype(v_ref.dtype)
    for h in range(hb):
        k_ref[h, :, :QK_NOPE] = acc[:, h * hw:h * hw + QK_NOPE].astype(k_ref.dtype)
        k_ref[h, :, QK_NOPE:] = kr
        v_ref[h, :V_DIM, :] = acc[:, h * hw + QK_NOPE:(h + 1) * hw].T.astype(v_ref.dtype)
        v_ref[h, V_DIM:, :] = ones_rows


def kv_b(ckv, krope, w_kv_b, hb=8, tm=1024):
    m = ckv.shape[0]
    hw = QK_NOPE + V_DIM
    heads = w_kv_b.shape[1] // hw
    hb = min(hb, heads)
    tm = _tile(m, tm)
    return pl.pallas_call(
        functools.partial(_kv_b_kernel, hb=hb),
        grid=(m // tm, heads // hb),
        in_specs=[pl.BlockSpec((tm, KV_RANK), lambda i, j: (i, 0)),
                  pl.BlockSpec((tm, QK_ROPE), lambda i, j: (i, 0)),
                  pl.BlockSpec((KV_RANK, hb * hw), lambda i, j: (0, j))],
        out_specs=[pl.BlockSpec((hb, tm, QK_NOPE + QK_ROPE), lambda i, j: (j, i, 0)),
                   pl.BlockSpec((hb, V_DIM + V_PAD, tm), lambda i, j: (j, 0, i))],
        out_shape=[jax.ShapeDtypeStruct((heads, m, QK_NOPE + QK_ROPE), BF16),
                   jax.ShapeDtypeStruct((heads, V_DIM + V_PAD, m), BF16)],
        compiler_params=_cparams("parallel", "parallel"),
        name="kv_b",
    )(ckv, krope, w_kv_b)


def _q_a_kernel(a_ref, w_ref, g_ref, o_ref):
    c = jnp.dot(a_ref[...], w_ref[...], preferred_element_type=F32)
    c = c * lax.rsqrt(jnp.mean(c * c, axis=-1, keepdims=True) + EPS)
    o_ref[...] = (c * g_ref[...]).astype(o_ref.dtype)


def q_a(a, w, layer, g, tm=512):
    m, kd = a.shape
    n = w.shape[2]
    tm = _tile(m, tm)
    return pl.pallas_call(
        _q_a_kernel,
        grid=(m // tm,),
        in_specs=[pl.BlockSpec((tm, kd), lambda i: (i, 0)),
                  pl.BlockSpec((None, kd, n), lambda i: (layer, 0, 0)),
                  pl.BlockSpec((1, n), lambda i: (0, 0))],
        out_specs=pl.BlockSpec((tm, n), lambda i: (i, 0)),
        out_shape=jax.ShapeDtypeStruct((m, n), BF16),
        compiler_params=_cparams("parallel"),
        name="q_a",
    )(a, w, g.reshape(1, n).astype(F32))


def _q_b_kernel(c_ref, w_ref, cos_ref, sin_ref, o_ref, *, hb, scale, nsub):
    reps = hb * QK_ROPE // LANES
    rs = c_ref.shape[0] // nsub
    for r in range(nsub):
        rows = slice(r * rs, (r + 1) * rs)
        acc = jnp.dot(c_ref[rows, :], w_ref[...], preferred_element_type=F32)
        nope = acc[:, :hb * QK_NOPE] * scale
        pe = acc[:, hb * QK_NOPE:]
        cos = jnp.concatenate([cos_ref[rows, :]] * reps, axis=1)
        sin = jnp.concatenate([sin_ref[rows, :]] * reps, axis=1)
        pe = (pe * cos + _rotate_half_pairs(pe) * sin) * scale
        for h in range(hb):
            o_ref[h, rows, :QK_NOPE] = nope[:, h * QK_NOPE:(h + 1) * QK_NOPE].astype(o_ref.dtype)
            o_ref[h, rows, QK_NOPE:] = pe[:, h * QK_ROPE:(h + 1) * QK_ROPE].astype(o_ref.dtype)


def q_b(cq, w_q_b, cos_t, sin_t, hb=8, tm=1024):
    m, kd = cq.shape
    hd = QK_NOPE + QK_ROPE
    heads = w_q_b.shape[1] // hd
    hb = min(hb, heads)
    assert (hb * QK_ROPE) % LANES == 0
    w3 = w_q_b.reshape(kd, heads // hb, hb, hd)
    w = jnp.concatenate([w3[..., :QK_NOPE].reshape(kd, heads // hb, hb * QK_NOPE),
                         w3[..., QK_NOPE:].reshape(kd, heads // hb, hb * QK_ROPE)], axis=-1)
    gw = hb * hd
    w = w.reshape(kd, (heads // hb) * gw).astype(BF16)
    tm = _tile(m, tm)
    return pl.pallas_call(
        functools.partial(_q_b_kernel, hb=hb, scale=float(hd) ** -0.5 * math.log2(math.e),
                          nsub=4 if tm % 64 == 0 else 1),
        grid=(m // tm, heads // hb),
        in_specs=[pl.BlockSpec((tm, kd), lambda i, j: (i, 0)),
                  pl.BlockSpec((kd, gw), lambda i, j: (0, j)),
                  pl.BlockSpec((tm, LANES), lambda i, j: (i, 0)),
                  pl.BlockSpec((tm, LANES), lambda i, j: (i, 0))],
        out_specs=pl.BlockSpec((hb, tm, hd), lambda i, j: (j, i, 0)),
        out_shape=jax.ShapeDtypeStruct((heads, m, hd), BF16),
        compiler_params=_cparams("parallel", "parallel"),
        name="q_b",
    )(cq, w, cos_t, sin_t)


def _flash_kernel(q_ref, k_ref, vt_ref, o_ref, s0_ref, s1_ref, m_ref, acc_ref, *, tq, tk, nq):
    contract_last = (((1,), (1,)), ((), ()))
    bufs = (s0_ref, s1_ref)
    ratio = tq // tk
    blocks = [(qi, j, max(0, j - ratio * qi) * tk) for qi in range(nq) for j in range(ratio * qi + ratio)]

    def scores(qi, j, q_lo, s_ref):
        q = q_ref[0, qi * tq + q_lo:(qi + 1) * tq, :]
        k = k_ref[0, j * tk:(j + 1) * tk, :]
        s_ref[:, q_lo:] = lax.dot_general(k, q, contract_last, preferred_element_type=F32)

    def update(qi, j, q_lo, s_ref):
        vt = vt_ref[0, :, j * tk:(j + 1) * tk]
        s = s_ref[:, q_lo:]
        if j >= ratio * qi:
            key = j * tk + lax.broadcasted_iota(jnp.int32, s.shape, 0)
            qry = qi * tq + q_lo + lax.broadcasted_iota(jnp.int32, s.shape, 1)
            s = jnp.where(key <= qry, s, -jnp.inf)
        m_cur = jnp.max(s, axis=0, keepdims=True)
        if j == 0:
            p = jnp.exp2(s - m_cur)
            m_ref[...] = m_cur
            acc_ref[...] = jnp.dot(vt, p.astype(vt.dtype), preferred_element_type=F32)
        else:
            m_prev = m_ref[:, q_lo:]
            m_new = jnp.maximum(m_prev, m_cur)
            p = jnp.exp2(s - m_new)
            alpha = jnp.exp2(m_prev - m_new)
            acc_ref[:, q_lo:] = (alpha * acc_ref[:, q_lo:]
                                 + jnp.dot(vt, p.astype(vt.dtype), preferred_element_type=F32))
            m_ref[:, q_lo:] = m_new
        if j == ratio * qi + ratio - 1:
            acc = acc_ref[...]
            o_ref[qi * tq:(qi + 1) * tq, :] = (acc[:V_DIM] / acc[V_DIM:V_DIM + 1]).T.astype(o_ref.dtype)

    scores(*blocks[0], bufs[0])
    for n, blk in enumerate(blocks):
        if n + 1 < len(blocks):
            scores(*blocks[n + 1], bufs[(n + 1) % 2])
        update(*blk, bufs[n % 2])


def flash_attention(q, k, vt, bsz, seq_len, tq=1024, tk=1024):
    heads, t, hd = q.shape
    vrows = vt.shape[1]
    tq = _tile(seq_len, tq)
    tk = _tile(tq, tk)
    nq = seq_len // tq
    kern = functools.partial(_flash_kernel, tq=tq, tk=tk, nq=nq)
    return pl.pallas_call(
        kern,
        grid=(bsz, heads),
        in_specs=[pl.BlockSpec((1, seq_len, hd), lambda b, h: (h, b, 0)),
                  pl.BlockSpec((1, seq_len, hd), lambda b, h: (h, b, 0)),
                  pl.BlockSpec((1, vrows, seq_len), lambda b, h: (h, 0, b))],
        out_specs=pl.BlockSpec((seq_len, V_DIM), lambda b, h: (b, h)),
        out_shape=jax.ShapeDtypeStruct((t, heads * V_DIM), BF16),
        scratch_shapes=[pltpu.VMEM((tk, tq), F32), pltpu.VMEM((tk, tq), F32),
                        pltpu.VMEM((1, tq), F32), pltpu.VMEM((vrows, tq), F32)],
        compiler_params=_cparams("parallel", "parallel"),
        name="flash_attention",
    )(q, k, vt)


def kernel(x, positions, ln_mix, ln_ffn, ln_final, ssm_A_re, ssm_A_im, ssm_log_dt, ssm_B_re, ssm_B_im,
           ssm_C_re, ssm_C_im, ssm_D, ssm_w_glu, ssm_b_glu, kv_in_norm, w_kv_a, kv_latent_norm, w_kv_b,
           w_q_a, q_latent_norm, w_q_b, w_o, ffn_w_in, ffn_conv_w, ffn_conv_b, ffn_w_out):
    bsz, seq_len, d = x.shape
    t = bsz * seq_len
    depth = ln_mix.shape[0]
    n_a = ssm_A_re.shape[0]

    xs = x.reshape(t, d).astype(F32)
    cos_t, sin_t = rope_tables(positions)
    k_heads = v_heads = None
    w_kv_b_bf = w_kv_b.astype(BF16)
    w_glu_bf, w_q_a_bf, w_o_bf, w_out_bf = (w.astype(BF16) for w in (ssm_w_glu, w_q_a, w_o, ffn_w_out))
    for l in range(depth):
        h = rmsnorm(xs, ln_mix[l], BF16, transposed=l < n_a)
        if l < n_a:
            y = s5_mixer_pre_glu(h, bsz, seq_len, ssm_A_re[l], ssm_A_im[l], ssm_log_dt[l], ssm_B_re[l],
                                 ssm_B_im[l], ssm_C_re[l], ssm_C_im[l], ssm_D[l])
            xs = mm_glu(y, w_glu_bf, l, ssm_b_glu[l], xs)
        else:
            b = l - n_a
            cq = q_a(h, w_q_a_bf, b, q_latent_norm[b])
            q_heads = q_b(cq, w_q_b[b], cos_t, sin_t)
            o = flash_attention(q_heads, k_heads, v_heads, bsz, seq_len)
            xs = mm_residual(o, w_o_bf, b, xs)
        act = ffn_in(rmsnorm(xs, ln_ffn[l], BF16), ffn_w_in, l, ffn_conv_w[l], ffn_conv_b[l], seq_len)
        xs = mm_residual(act, w_out_bf, l, xs)
        if l == n_a - 1:
            ckv, krope = kv_a(rmsnorm(xs, kv_in_norm, BF16), w_kv_a, kv_latent_norm, cos_t, sin_t)
            k_heads, v_heads = kv_b(ckv, krope, w_kv_b_bf)
    return rmsnorm(xs, ln_final, x.dtype).reshape(bsz, seq_len, d)
```
